```python
import math
import jax, jax.numpy as jnp
from jax import lax
import numpy as np

D_MODEL = 1024
BATCH = 8
SEQ = 2048
DEPTH = 2
DEC_BATCH = 32
DEC_SEQ = 8
PAST_LEN = 8192
PAGE_SIZE = 128

N_EVEN = (DEPTH + 1) // 2
N_ODD = DEPTH // 2

RET_HEADS = 8
RET_DK = 64
RET_DV = 64
RET_CHUNK = 128
ROPE_BASE = 10000.0
RWKV_HEADS = 8
RWKV_N = 64
RWKV_W_RANK = 64
RWKV_A_RANK = 64
RWKV_G_RANK = 128
SB_HEADS = 16
SB_DH = 64
SB_BLOCK = 128
N_EXPERTS = 64
TOP_K = 8
N_GROUPS = 8
TOPK_GROUPS = 4
EXPERT_HIDDEN = 256
SHARED_HIDDEN = 256
ROUTED_SCALE = 2.5
EPS = 1e-6

RET_W = RET_HEADS * RET_DK
RWKV_W = RWKV_HEADS * RWKV_N
A_COLS = 4 * RET_W
B_COLS = 3 * RWKV_W + RWKV_W_RANK + RWKV_A_RANK + RWKV_G_RANK
AB_COLS = A_COLS + B_COLS
MIX_W = RET_W + RWKV_W
SB_W = SB_HEADS * SB_DH

kernel_name = 'hybrid_retention_rwkv7_stickbreaking_moe_step'


def rms_norm(x, g):
    xf = x.astype(jnp.float32)
    y = xf * lax.rsqrt(jnp.mean(xf * xf, axis=-1, keepdims=True) + EPS)
    return y.astype(x.dtype) * g


def modulate(x, g, shift, scale):
    return rms_norm(x, g) * (1 + scale[:, None, :]) + shift[:, None, :]


def head_norm(y, w, b, eps):
    yf = y.astype(jnp.float32)
    mu = jnp.mean(yf, axis=-1, keepdims=True)
    var = jnp.mean(jnp.square(yf - mu), axis=-1, keepdims=True)
    return ((yf - mu) * lax.rsqrt(var + eps)).astype(y.dtype) * w + b


def rotary(x, pos):
    half = x.shape[-1] // 2
    inv = ROPE_BASE ** (-jnp.arange(half, dtype=jnp.float32) / half)
    ang = pos.astype(jnp.float32)[:, None] * inv[None, :]
    cos = jnp.cos(ang)[None, :, None, :].astype(x.dtype)
    sin = jnp.sin(ang)[None, :, None, :].astype(x.dtype)
    x1, x2 = x[..., :half], x[..., half:]
    return jnp.concatenate([x1 * cos - x2 * sin, x1 * sin + x2 * cos], axis=-1)


def retention(q, k, v, s0):
    B, L, H, _ = q.shape
    C = math.gcd(L, RET_CHUNK)
    n = L // C
    log_g = jnp.log1p(-jnp.exp2(-5.0 - jnp.arange(H, dtype=jnp.float32)))
    i = jnp.arange(C, dtype=jnp.float32)
    diff = i[:, None] - i[None, :]
    dmask = jnp.where(diff >= 0, jnp.exp(log_g[:, None, None] * jnp.maximum(diff, 0.0)), 0.0)
    q_dec = jnp.exp(log_g[:, None] * (i[None, :] + 1.0))
    k_dec = jnp.exp(log_g[:, None] * (C - 1.0 - i[None, :]))
    c_dec = jnp.exp(log_g * C)

    def to_chunks(t):
        return jnp.moveaxis(t.astype(jnp.float32).reshape(B, n, C, H, t.shape[-1]), 1, 0)

    def step(s, inp):
        qc, kc, vc = inp
        sc = jnp.einsum('bihd,bjhd->bhij', qc, kc) * dmask
        o = jnp.einsum('bhij,bjhe->bihe', sc, vc)
        o = o + jnp.einsum('bihd,hi,bhde->bihe', qc, q_dec, s)
        s = s * c_dec[None, :, None, None] + jnp.einsum('bjhd,hj,bjhe->bhde', kc, k_dec, vc)
        return s, o

    s, o = lax.scan(step, s0.astype(jnp.float32), (to_chunks(q), to_chunks(k), to_chunks(v)))
    o = jnp.moveaxis(o, 0, 1).reshape(B, L, H, -1)
    return o.astype(q.dtype), s.astype(s0.dtype)


def rwkv7_scan(r, decay, k, v, kk, a, s0):
    def f(t):
        return jnp.moveaxis(t.astype(jnp.float32), 1, 0)

    def step(s, inp):
        rt, wt, kt, vt, kkt, at = inp
        sa = jnp.einsum('bhvk,bhk->bhv', s, -kkt)
        s = s * wt[:, :, None, :] + sa[..., None] * (kkt * at)[:, :, None, :] + vt[..., None] * kt[:, :, None, :]
        return s, jnp.einsum('bhvk,bhk->bhv', s, rt)

    s, y = lax.scan(step, s0.astype(jnp.float32), tuple(f(t) for t in (r, decay, k, v, kk, a)))
    return jnp.moveaxis(y, 0, 1).astype(r.dtype), s.astype(s0.dtype)


def token_shift(p, prev, mu):
    p_prev = jnp.concatenate([prev[:, None, :], p[:, :-1]], axis=1)
    return p + (p_prev - p) * mu


def mix_ab(h, pos, s_ret, s_rwkv, shift_prev, w_in, w_out, ret_gn_w, ret_gn_b, mu, w0, w_up, a0, a_up,
           g_up, k_k, k_a, r_k, ln_w, ln_b):
    B, L, _ = h.shape
    p = h @ w_in
    pa, pb = p[..., :A_COLS], p[..., A_COLS:]
    q, k, v, g = jnp.split(pa, 4, axis=-1)
    q = rotary(q.reshape(B, L, RET_HEADS, RET_DK), pos)
    k = rotary(k.reshape(B, L, RET_HEADS, RET_DK), pos) * (RET_DK ** -0.5)
    v = v.reshape(B, L, RET_HEADS, RET_DV)
    o_a, s_ret_new = retention(q, k, v, s_ret)
    o_a = head_norm(o_a, ret_gn_w.reshape(RET_HEADS, RET_DV), ret_gn_b.reshape(RET_HEADS, RET_DV), 1e-5)
    o_a = o_a.reshape(B, L, RET_W) * jax.nn.silu(g)
    pbs = token_shift(pb, shift_prev, mu)
    r, kb, vb, wd, ad, gd = jnp.split(
        pbs, [RWKV_W, 2 * RWKV_W, 3 * RWKV_W, 3 * RWKV_W + RWKV_W_RANK, 3 * RWKV_W + RWKV_W_RANK + RWKV_A_RANK], axis=-1)
    w_raw = -jax.nn.softplus(-(w0 + jnp.tanh(wd) @ w_up)) - 0.5
    decay = jnp.exp(-jnp.exp(w_raw.astype(jnp.float32)))
    a = jax.nn.sigmoid(a0 + ad @ a_up)
    gb = jax.nn.sigmoid(gd) @ g_up

    def hd(t):
        return t.reshape(B, L, RWKV_HEADS, RWKV_N)

    kkf = hd(kb * k_k).astype(jnp.float32)
    kk = kkf * lax.rsqrt(jnp.maximum(jnp.sum(kkf * kkf, axis=-1, keepdims=True), 1e-24))
    kb = kb * (1 + (a - 1) * k_a)
    y, s_rwkv_new = rwkv7_scan(hd(r), hd(decay), hd(kb), hd(vb), kk, hd(a), s_rwkv)
    y = head_norm(y, ln_w.reshape(RWKV_HEADS, RWKV_N), ln_b.reshape(RWKV_HEADS, RWKV_N), 64e-5)
    y = y + jnp.sum(hd(r) * hd(kb) * r_k, axis=-1, keepdims=True) * hd(vb)
    o_b = y.reshape(B, L, RWKV_W) * gb
    out = jnp.concatenate([o_a, o_b], axis=-1) @ w_out
    return out, s_ret_new, s_rwkv_new, pb[:, -1]


def stick_breaking(q, k, v, q_pos, k_pos):
    z = jnp.einsum('bqhd,bkhd->bhqk', q, k).astype(jnp.float32) * (SB_DH ** -0.5)
    mask = k_pos[None, :] < q_pos[:, None]
    log_beta = jax.nn.log_sigmoid(z)
    log_rem = jnp.where(mask, jax.nn.log_sigmoid(-z), 0.0)
    after = lax.cumsum(log_rem, axis=3, reverse=True) - log_rem
    w = jnp.where(mask, jnp.exp(log_beta + after), 0.0)
    return jnp.einsum('bhqk,bkhd->bqhd', w.astype(v.dtype), v)


def sb_sweep(q, k, v, past):
    L = q.shape[1]
    outs = []
    for b0 in range(0, L, SB_BLOCK):
        b1 = min(b0 + SB_BLOCK, L)
        n_keys = past + b1
        outs.append(stick_breaking(q[:, b0:b1], k[:, :n_keys], v[:, :n_keys],
                                   past + jnp.arange(b0, b1), jnp.arange(n_keys)))
    return jnp.concatenate(outs, axis=1)


def mix_c(h, past_k, past_v, w_qkv, b_q, b_k, w_out):
    B, L, _ = h.shape
    q, k, v = jnp.split(h @ w_qkv, 3, axis=-1)
    q = (q + b_q).reshape(B, L, SB_HEADS, SB_DH)
    k = (k + b_k).reshape(B, L, SB_HEADS, SB_DH)
    v = v.reshape(B, L, SB_HEADS, SB_DH)
    if past_k is None:
        k_all, v_all, past = k, v, 0
    else:
        k_all = jnp.concatenate([past_k, k], axis=1)
        v_all = jnp.concatenate([past_v, v], axis=1)
        past = past_k.shape[1]
    o = sb_sweep(q, k_all, v_all, past)
    return o.reshape(B, L, SB_W) @ w_out, k, v


def swiglu(t, wg, wu, wd):
    return (jax.nn.silu(t @ wg) * (t @ wu)) @ wd


def moe(h, router_w, router_bias, w_gate, w_up, w_down, sh_gate, sh_up, sh_down):
    B, L, D = h.shape
    t = h.reshape(B * L, D)
    scores = jax.nn.sigmoid((t @ router_w).astype(jnp.float32))
    biased = scores + router_bias.astype(jnp.float32)
    grp = biased.reshape(-1, N_GROUPS, N_EXPERTS // N_GROUPS)
    grp_score = jnp.sum(lax.top_k(grp, 2)[0], axis=-1)
    _, gidx = lax.top_k(grp_score, TOPK_GROUPS)
    gmask = jnp.any(gidx[:, :, None] == jnp.arange(N_GROUPS)[None, None, :], axis=1)
    emask = jnp.repeat(gmask, N_EXPERTS // N_GROUPS, axis=1)
    _, eidx = lax.top_k(jnp.where(emask, biased, -jnp.inf), TOP_K)
    sel = jnp.take_along_axis(scores, eidx, axis=1)
    wts = sel / jnp.sum(sel, axis=-1, keepdims=True) * ROUTED_SCALE
    comb = jnp.sum(jax.nn.one_hot(eidx, N_EXPERTS, dtype=jnp.float32) * wts[..., None], axis=1)

    def expert(acc, inp):
        wg, wu, wd, c = inp
        return acc + c[:, None] * swiglu(t, wg, wu, wd), None

    y0 = swiglu(t, sh_gate, sh_up, sh_down)
    y, _ = lax.scan(expert, y0, (w_gate, w_up, w_down, comb.T.astype(t.dtype)))
    return y.reshape(B, L, D)


def trunk(x, c, pos, ret_s, rwkv_s, shift_s, past_k, past_v, p):
    new_ret, new_rwkv, new_shift, new_k, new_v = [], [], [], [], []
    cs = jax.nn.silu(c)
    for l in range(DEPTH):
        mod = cs @ p['ada_w'][l] + p['ada_b'][l]
        sh1, sc1, g1, sh2, sc2, g2 = jnp.split(mod, 6, axis=-1)
        h = modulate(x, p['norm_mix'][l], sh1, sc1)
        i = l // 2
        if l % 2 == 0:
            o, sr, sw, ss = mix_ab(h, pos, ret_s[i], rwkv_s[i], shift_s[i], p['w_in_ab'][i], p['w_out_ab'][i],
                                   p['ret_gn_w'][i], p['ret_gn_b'][i], p['rwkv_mu'][i], p['rwkv_w0'][i],
                                   p['rwkv_w_up'][i], p['rwkv_a0'][i], p['rwkv_a_up'][i], p['rwkv_g_up'][i],
                                   p['rwkv_k_k'][i], p['rwkv_k_a'][i], p['rwkv_r_k'][i], p['rwkv_ln_w'][i],
                                   p['rwkv_ln_b'][i])
            new_ret.append(sr)
            new_rwkv.append(sw)
            new_shift.append(ss)
        else:
            pk = None if past_k is None else past_k[i]
            pv = None if past_v is None else past_v[i]
            o, kn, vn = mix_c(h, pk, pv, p['w_qkv_c'][i], p['b_q_c'][i], p['b_k_c'][i], p['w_out_c'][i])
            new_k.append(kn)
            new_v.append(vn)
        x = x + g1[:, None, :] * o
        h = modulate(x, p['norm_ffn'][l], sh2, sc2)
        x = x + g2[:, None, :] * moe(h, p['router_w'][l], p['router_bias'][l], p['exp_w_gate'][l],
                                     p['exp_w_up'][l], p['exp_w_down'][l], p['sh_w_gate'][l],
                                     p['sh_w_up'][l], p['sh_w_down'][l])
    y = rms_norm(x, p['final_norm'])
    return y, jnp.stack(new_ret), jnp.stack(new_rwkv), jnp.stack(new_shift), jnp.stack(new_k), jnp.stack(new_v)


def setup_inputs(seed: int = 0) -> dict:
    key = jax.random.key(seed)
    keys = jax.random.split(key, 64)
    ctr = [0]

    def nk():
        ctr[0] += 1
        return keys[ctr[0] - 1]

    def nrm(shape, s=1.0):
        return s * jax.random.normal(nk(), shape, jnp.float32)

    n_pages = PAST_LEN // PAGE_SIZE
    used = DEC_BATCH * n_pages
    n_pool = used + max(1, used // 4)
    page_table = jax.random.permutation(nk(), n_pool)[:used].reshape(DEC_BATCH, n_pages).astype(jnp.int32)
    D = D_MODEL
    sb_u = nrm((N_ODD, SB_W))
    b_k_c = sb_u + nrm((N_ODD, SB_W), 0.1)
    b_q_c = -sb_u + nrm((N_ODD, SB_W), 0.1)
    cache_k = nrm((N_ODD, n_pool, PAGE_SIZE, SB_HEADS, SB_DH)) + b_k_c.reshape(N_ODD, 1, 1, SB_HEADS, SB_DH)
    return {
        'x_prompt': nrm((BATCH, SEQ, D)),
        'x_sample': nrm((DEC_BATCH, DEC_SEQ, D)),
        'c_prompt': nrm((BATCH, D)),
        'c_sample': nrm((DEC_BATCH, D)),
        'state_ret': nrm((N_EVEN, DEC_BATCH, RET_HEADS, RET_DK, RET_DV), 0.3),
        'state_rwkv': nrm((N_EVEN, DEC_BATCH, RWKV_HEADS, RWKV_N, RWKV_N), 0.3),
        'state_shift': nrm((N_EVEN, DEC_BATCH, B_COLS), 0.5),
        'cache_k': cache_k,
        'cache_v': nrm((N_ODD, n_pool, PAGE_SIZE, SB_HEADS, SB_DH)),
        'page_table': page_table,
        'ada_w': nrm((DEPTH, D, 6 * D), 0.5 * D ** -0.5),
        'ada_b': nrm((DEPTH, 6 * D), 0.02),
        'norm_mix': 1.0 + nrm((DEPTH, D), 0.02),
        'norm_ffn': 1.0 + nrm((DEPTH, D), 0.02),
        'final_norm': 1.0 + nrm((D,), 0.02),
        'w_in_ab': nrm((N_EVEN, D, AB_COLS), D ** -0.5),
        'w_out_ab': nrm((N_EVEN, MIX_W, D), MIX_W ** -0.5),
        'ret_gn_w': 1.0 + nrm((N_EVEN, RET_W), 0.02),
        'ret_gn_b': nrm((N_EVEN, RET_W), 0.02),
        'rwkv_mu': jax.random.uniform(nk(), (N_EVEN, B_COLS), jnp.float32),
        'rwkv_w0': jnp.linspace(-5.0, -0.5, RWKV_W, dtype=jnp.float32)[None, :] + nrm((N_EVEN, RWKV_W), 0.1),
        'rwkv_w_up': nrm((N_EVEN, RWKV_W_RANK, RWKV_W), 0.5 * RWKV_W_RANK ** -0.5),
        'rwkv_a0': nrm((N_EVEN, RWKV_W), 0.1),
        'rwkv_a_up': nrm((N_EVEN, RWKV_A_RANK, RWKV_W), 0.5 * RWKV_A_RANK ** -0.5),
        'rwkv_g_up': nrm((N_EVEN, RWKV_G_RANK, RWKV_W), RWKV_G_RANK ** -0.5),
        'rwkv_k_k': 0.85 + nrm((N_EVEN, RWKV_W), 0.02),
        'rwkv_k_a': 1.0 + nrm((N_EVEN, RWKV_W), 0.02),
        'rwkv_r_k': nrm((N_EVEN, RWKV_HEADS, RWKV_N), 0.1),
        'rwkv_ln_w': 1.0 + nrm((N_EVEN, RWKV_W), 0.02),
        'rwkv_ln_b': nrm((N_EVEN, RWKV_W), 0.02),
        'w_qkv_c': nrm((N_ODD, D, 3 * SB_W), D ** -0.5),
        'b_q_c': b_q_c,
        'b_k_c': b_k_c,
        'w_out_c': nrm((N_ODD, SB_W, D), SB_W ** -0.5),
        'router_w': nrm((DEPTH, D, N_EXPERTS), D ** -0.5),
        'router_bias': nrm((DEPTH, N_EXPERTS), 0.01),
        'exp_w_gate': nrm((DEPTH, N_EXPERTS, D, EXPERT_HIDDEN), D ** -0.5),
        'exp_w_up': nrm((DEPTH, N_EXPERTS, D, EXPERT_HIDDEN), D ** -0.5),
        'exp_w_down': nrm((DEPTH, N_EXPERTS, EXPERT_HIDDEN, D), EXPERT_HIDDEN ** -0.5),
        'sh_w_gate': nrm((DEPTH, D, SHARED_HIDDEN), D ** -0.5),
        'sh_w_up': nrm((DEPTH, D, SHARED_HIDDEN), D ** -0.5),
        'sh_w_down': nrm((DEPTH, SHARED_HIDDEN, D), SHARED_HIDDEN ** -0.5),
    }


def reference(x_prompt, x_sample, c_prompt, c_sample, state_ret, state_rwkv, state_shift, cache_k, cache_v,
              page_table, ada_w, ada_b, norm_mix, norm_ffn, final_norm, w_in_ab, w_out_ab, ret_gn_w, ret_gn_b,
              rwkv_mu, rwkv_w0, rwkv_w_up, rwkv_a0, rwkv_a_up, rwkv_g_up, rwkv_k_k, rwkv_k_a, rwkv_r_k,
              rwkv_ln_w, rwkv_ln_b, w_qkv_c, b_q_c, b_k_c, w_out_c, router_w, router_bias, exp_w_gate, exp_w_up,
              exp_w_down, sh_w_gate, sh_w_up, sh_w_down):
    params = dict(ada_w=ada_w, ada_b=ada_b, norm_mix=norm_mix, norm_ffn=norm_ffn, final_norm=final_norm,
                  w_in_ab=w_in_ab, w_out_ab=w_out_ab, ret_gn_w=ret_gn_w, ret_gn_b=ret_gn_b, rwkv_mu=rwkv_mu,
                  rwkv_w0=rwkv_w0, rwkv_w_up=rwkv_w_up, rwkv_a0=rwkv_a0, rwkv_a_up=rwkv_a_up,
                  rwkv_g_up=rwkv_g_up, rwkv_k_k=rwkv_k_k, rwkv_k_a=rwkv_k_a, rwkv_r_k=rwkv_r_k,
                  rwkv_ln_w=rwkv_ln_w, rwkv_ln_b=rwkv_ln_b, w_qkv_c=w_qkv_c, b_q_c=b_q_c, b_k_c=b_k_c,
                  w_out_c=w_out_c, router_w=router_w, router_bias=router_bias, exp_w_gate=exp_w_gate,
                  exp_w_up=exp_w_up, exp_w_down=exp_w_down, sh_w_gate=sh_w_gate, sh_w_up=sh_w_up,
                  sh_w_down=sh_w_down)
    bp, lp = x_prompt.shape[0], x_prompt.shape[1]
    dt = x_prompt.dtype
    ret0 = jnp.zeros((N_EVEN, bp, RET_HEADS, RET_DK, RET_DV), dt)
    rwkv0 = jnp.zeros((N_EVEN, bp, RWKV_HEADS, RWKV_N, RWKV_N), dt)
    shift0 = jnp.zeros((N_EVEN, bp, B_COLS), dt)
    y_prompt, ret_p, rwkv_p, shift_p, k_p, v_p = trunk(
        x_prompt, c_prompt, jnp.arange(lp), ret0, rwkv0, shift0, None, None, params)
    bs, ls = x_sample.shape[0], x_sample.shape[1]
    n_pages = page_table.shape[1]
    past_len = n_pages * PAGE_SIZE
    past_k = cache_k[:, page_table].reshape(N_ODD, bs, past_len, SB_HEADS, SB_DH)
    past_v = cache_v[:, page_table].reshape(N_ODD, bs, past_len, SB_HEADS, SB_DH)
    y_sample, ret_s, rwkv_s, shift_s, k_s, v_s = trunk(
        x_sample, c_sample, past_len + jnp.arange(ls), state_ret, state_rwkv, state_shift, past_k, past_v, params)
    return (y_prompt, y_sample, ret_p, ret_s, rwkv_p, rwkv_s, shift_p, shift_s, k_p, v_p, k_s, v_s)
```

```python
import functools
import math

import jax
import jax.numpy as jnp
from jax import lax
from jax.experimental import pallas as pl
from jax.experimental.pallas import tpu as pltpu

F32 = jnp.float32
BF16 = jnp.bfloat16
HIGHEST = lax.Precision.HIGHEST

D_MODEL = 1024
DEPTH = 2
PAGE = 128
RET_HEADS = 8
RET_DK = 64
RWKV_HEADS = 8
RWKV_N = 64
RWKV_W = RWKV_HEADS * RWKV_N
RET_W = RET_HEADS * RET_DK
LOWRANK = 256
B_COLS = 3 * RWKV_W + LOWRANK
SB_HEADS = 16
SB_DH = 64
SB_W = SB_HEADS * SB_DH
N_EXPERTS = 64
N_GROUPS = 8
GROUP_SIZE = N_EXPERTS // N_GROUPS
TOPK_GROUPS = 4
TOP_K = 8
HIDDEN = 256
ROUTED_SCALE = 2.5
EPS = 1e-6
ROPE_BASE = 10000.0
LANES = 128
VMEM_LIMIT = 56 * 1024 * 1024

RWKV_CHUNK = 16


def _cparams(sem):
    return pltpu.CompilerParams(dimension_semantics=sem, vmem_limit_bytes=VMEM_LIMIT)


def _dot(a, b, hp=False):
    if hp:
        return jnp.dot(a.astype(F32), b.astype(F32), preferred_element_type=F32, precision=HIGHEST)
    return jnp.dot(a.astype(BF16), b.astype(BF16), preferred_element_type=F32)


def _dot_nt(a, b, hp=False):
    dn = (((1,), (1,)), ((), ()))
    if hp:
        return lax.dot_general(a.astype(F32), b.astype(F32), dn, preferred_element_type=F32, precision=HIGHEST)
    return lax.dot_general(a.astype(BF16), b.astype(BF16), dn, preferred_element_type=F32)


def _dot_tn(a, b, hp=False):
    dn = (((0,), (0,)), ((), ()))
    if hp:
        return lax.dot_general(a.astype(F32), b.astype(F32), dn, preferred_element_type=F32, precision=HIGHEST)
    return lax.dot_general(a.astype(BF16), b.astype(BF16), dn, preferred_element_type=F32)


def _silu(x):
    return x * jax.nn.sigmoid(x)


def _lo_mask(shape):
    return (lax.broadcasted_iota(jnp.int32, shape, len(shape) - 1) % LANES) < (LANES // 2)


def _pair_sum(x, lo):
    s_lo = jnp.sum(jnp.where(lo, x, 0.0), axis=-1, keepdims=True)
    s_hi = jnp.sum(jnp.where(lo, 0.0, x), axis=-1, keepdims=True)
    return jnp.where(lo, s_lo, s_hi)


def _ada_kernel(c_ref, w_ref, b_ref, o_ref):
    o_ref[0] = _dot(_silu(c_ref[...]), w_ref[0]) + b_ref[0]


def _ada(c, ada_w, ada_b):
    bsz = c.shape[0]
    n = ada_w.shape[-1]
    tn = D_MODEL
    return pl.pallas_call(
        _ada_kernel,
        grid=(DEPTH, n // tn),
        in_specs=[
            pl.BlockSpec((bsz, D_MODEL), lambda l, j: (0, 0)),
            pl.BlockSpec((1, D_MODEL, tn), lambda l, j: (l, 0, j)),
            pl.BlockSpec((1, 1, tn), lambda l, j: (l, 0, j)),
        ],
        out_specs=pl.BlockSpec((1, bsz, tn), lambda l, j: (l, 0, j)),
        out_shape=jax.ShapeDtypeStruct((DEPTH, bsz, n), F32),
        compiler_params=_cparams(("parallel", "parallel")),
        name="ada_mod",
    )(c, ada_w, ada_b.reshape(DEPTH, 1, n))


def _modulated_norm(x, g, sh, sc):
    y = x * lax.rsqrt(jnp.mean(x * x, axis=-1, keepdims=True) + EPS)
    return (y * g) * (1.0 + sc) + sh


def _norm_proj_kernel(n_out, x_ref, g_ref, sh_ref, sc_ref, *refs):
    w_refs, b_refs, o_refs = refs[:n_out], refs[n_out:2 * n_out], refs[2 * n_out:]
    h = _modulated_norm(x_ref[0], g_ref[...], sh_ref[0], sc_ref[0]).astype(BF16)
    for w_ref, b_ref, o_ref in zip(w_refs, b_refs, o_refs):
        o_ref[0] = jnp.dot(h, w_ref[...], preferred_element_type=F32) + b_ref[...]


def _norm_proj(x3, g, sh3, sc3, ws, bs, tl):
    gsz, lg, _ = x3.shape
    lm = sh3.shape[1]
    tm = 1 if lm == 1 else tl
    mod_map = (lambda b, i: (b, 0, 0)) if lm == 1 else (lambda b, i: (b, i, 0))
    n_out = len(ws)
    in_specs = [
        pl.BlockSpec((1, tl, D_MODEL), lambda b, i: (b, i, 0)),
        pl.BlockSpec((1, D_MODEL), lambda b, i: (0, 0)),
        pl.BlockSpec((1, tm, D_MODEL), mod_map),
        pl.BlockSpec((1, tm, D_MODEL), mod_map),
    ]
    in_specs += [pl.BlockSpec(w.shape, lambda b, i: (0, 0)) for w in ws]
    in_specs += [pl.BlockSpec(b.shape, lambda b, i: (0, 0)) for b in bs]
    return pl.pallas_call(
        functools.partial(_norm_proj_kernel, n_out),
        grid=(gsz, lg // tl),
        in_specs=in_specs,
        out_specs=[pl.BlockSpec((1, tl, w.shape[1]), lambda b, i: (b, i, 0)) for w in ws],
        out_shape=[jax.ShapeDtypeStruct((gsz, lg, w.shape[1]), F32) for w in ws],
        compiler_params=_cparams(("parallel", "parallel")),
        name="norm_proj",
    )(x3, g.reshape(1, D_MODEL), sh3, sc3, *ws, *bs)


def _out_proj_kernel(n_in, x_ref, gate_ref, *refs):
    o_refs, w_refs, y_ref = refs[:n_in], refs[n_in:2 * n_in], refs[2 * n_in]
    acc = None
    for o_ref, w_ref in zip(o_refs, w_refs):
        t = jnp.dot(o_ref[0].astype(BF16), w_ref[...], preferred_element_type=F32)
        acc = t if acc is None else acc + t
    y_ref[0] = x_ref[0] + gate_ref[0] * acc


def _out_proj(x3, gate3, os_, ws, tl):
    gsz, lg, _ = x3.shape
    lm = gate3.shape[1]
    tm = 1 if lm == 1 else tl
    mod_map = (lambda b, i: (b, 0, 0)) if lm == 1 else (lambda b, i: (b, i, 0))
    n_in = len(os_)
    in_specs = [
        pl.BlockSpec((1, tl, D_MODEL), lambda b, i: (b, i, 0)),
        pl.BlockSpec((1, tm, D_MODEL), mod_map),
    ]
    in_specs += [pl.BlockSpec((1, tl, o.shape[-1]), lambda b, i: (b, i, 0)) for o in os_]
    in_specs += [pl.BlockSpec(w.shape, lambda b, i: (0, 0)) for w in ws]
    return pl.pallas_call(
        functools.partial(_out_proj_kernel, n_in),
        grid=(gsz, lg // tl),
        in_specs=in_specs,
        out_specs=pl.BlockSpec((1, tl, D_MODEL), lambda b, i: (b, i, 0)),
        out_shape=jax.ShapeDtypeStruct(x3.shape, F32),
        compiler_params=_cparams(("parallel", "parallel")),
        name="out_proj",
    )(x3, gate3, *os_, *ws)


def _rot_half(x, lo32):
    return jnp.where(lo32, pltpu.roll(x, LANES - 32, 1), pltpu.roll(x, 32, 1))


def _retention_kernel(chunk, lgs_ref, q_ref, k_ref, v_ref, g_ref, cos_ref, sin_ref, gw_ref, gb_ref, s0_ref,
                      o_ref, s_ref, st_ref):
    hp = pl.program_id(1)
    c = pl.program_id(2)

    @pl.when(c == 0)
    def _():
        st_ref[...] = s0_ref[0, 0]

    lane = lax.broadcasted_iota(jnp.int32, (chunk, LANES), 1)
    lo = lane < (LANES // 2)
    lo32 = (lane % RET_DK) < (RET_DK // 2)
    lg_a = lgs_ref[2 * hp]
    lg_b = lgs_ref[2 * hp + 1]
    lg = jnp.where(lo, lg_a, lg_b)

    cos = cos_ref[...]
    sin = sin_ref[...]
    q = q_ref[0]
    k = k_ref[0]
    q = q * cos + _rot_half(q, lo32) * sin
    k = (k * cos + _rot_half(k, lo32) * sin) * (RET_DK ** -0.5)
    v = v_ref[0]

    ti = lax.broadcasted_iota(jnp.int32, (chunk, LANES), 0).astype(F32)
    q_dec = jnp.exp(lg * (ti + 1.0))
    k_dec = jnp.exp(lg * (chunk - 1.0 - ti))

    ii = lax.broadcasted_iota(jnp.int32, (chunk, chunk), 0)
    jj = lax.broadcasted_iota(jnp.int32, (chunk, chunk), 1)
    diff = jnp.maximum(ii - jj, 0).astype(F32)
    causal = ii >= jj
    kb = k.astype(BF16)
    vb = v.astype(BF16)

    def head(lg_h, sel):
        dmask = jnp.where(causal, jnp.exp(lg_h * diff), 0.0)
        qh = jnp.where(sel, q, 0.0)
        sc = _dot_nt(qh, kb) * dmask
        return _dot(sc, vb)

    o = jnp.where(lo, head(lg_a, lo), head(lg_b, jnp.logical_not(lo)))
    st = st_ref[...]
    o = o + _dot(q * q_dec, st)

    ri = lax.broadcasted_iota(jnp.int32, (LANES, LANES), 0)
    ci = lax.broadcasted_iota(jnp.int32, (LANES, LANES), 1)
    same_head = (ri < RET_DK) == (ci < RET_DK)
    c_dec = jnp.exp(jnp.where(ri < RET_DK, lg_a, lg_b) * float(chunk))
    st_new = st * c_dec + jnp.where(same_head, _dot_tn(k * k_dec, vb), 0.0)
    st_ref[...] = st_new

    mu = _pair_sum(o, lo) * (1.0 / RET_DK)
    d = o - mu
    var = _pair_sum(d * d, lo) * (1.0 / RET_DK)
    y = (d * lax.rsqrt(var + 1e-5)) * gw_ref[...] + gb_ref[...]
    o_ref[0] = y * _silu(g_ref[0])

    @pl.when(c == pl.num_programs(2) - 1)
    def _():
        s_ref[0, 0] = st_new


def _retention(pa, cos_t, sin_t, lgs, gn_w, gn_b, s0_bd, chunk):
    bsz, seq, _ = pa.shape
    npair = RET_HEADS // 2
    nc = seq // chunk
    col = lambda off: (lambda b, h, c: (b, c, off + h))
    return pl.pallas_call(
        functools.partial(_retention_kernel, chunk),
        grid=(bsz, npair, nc),
        in_specs=[
            pl.BlockSpec(memory_space=pltpu.SMEM),
            pl.BlockSpec((1, chunk, LANES), col(0)),
            pl.BlockSpec((1, chunk, LANES), col(npair)),
            pl.BlockSpec((1, chunk, LANES), col(2 * npair)),
            pl.BlockSpec((1, chunk, LANES), col(3 * npair)),
            pl.BlockSpec((chunk, LANES), lambda b, h, c: (c, 0)),
            pl.BlockSpec((chunk, LANES), lambda b, h, c: (c, 0)),
            pl.BlockSpec((1, LANES), lambda b, h, c: (0, h)),
            pl.BlockSpec((1, LANES), lambda b, h, c: (0, h)),
            pl.BlockSpec((1, 1, LANES, LANES), lambda b, h, c: (b, h, 0, 0)),
        ],
        out_specs=[
            pl.BlockSpec((1, chunk, LANES), lambda b, h, c: (b, c, h)),
            pl.BlockSpec((1, 1, LANES, LANES), lambda b, h, c: (b, h, 0, 0)),
        ],
        out_shape=[
            jax.ShapeDtypeStruct((bsz, seq, RET_W), F32),
            jax.ShapeDtypeStruct((bsz, npair, LANES, LANES), F32),
        ],
        scratch_shapes=[pltpu.VMEM((LANES, LANES), F32)],
        compiler_params=_cparams(("parallel", "parallel", "arbitrary")),
        name="retention",
    )(lgs, pa, pa, pa, pa, cos_t, sin_t, gn_w.reshape(1, RET_W), gn_b.reshape(1, RET_W), s0_bd)


def _rwkv_kernel(tb, chunk, pb_ref, prev_ref, s0_ref, mu_ref, wlr_ref, w0_ref, a0_ref, kk_ref, ka_ref, rk_ref,
                 lnw_ref, lnb_ref, o_ref, s_ref,
                 st_ref, carry_ref, r_s, lw_s, kb_s, v_s, kn_s, bv_s, y_s):
    npair = RWKV_HEADS // 2
    blk = pl.program_id(1)

    @pl.when(blk == 0)
    def _():
        st_ref[...] = s0_ref[0]
        carry_ref[...] = prev_ref[0]

    pb = pb_ref[0]
    row = lax.broadcasted_iota(jnp.int32, pb.shape, 0)
    prev = jnp.where(row == 0, carry_ref[...], pltpu.roll(pb, 1, 0))
    carry_ref[...] = pb[tb - 1:tb, :]
    pbs = pb + (prev - pb) * mu_ref[...]
    r = pbs[:, :RWKV_W]
    kb = pbs[:, RWKV_W:2 * RWKV_W]
    v = pbs[:, 2 * RWKV_W:3 * RWKV_W]
    tail = pbs[:, 3 * RWKV_W:]
    tl_lane = lax.broadcasted_iota(jnp.int32, tail.shape, 1)
    act = jnp.where(tl_lane < 64, jnp.tanh(tail), jnp.where(tl_lane < 128, tail, jax.nn.sigmoid(tail)))
    lr = _dot(act, wlr_ref[...])
    wz = -(w0_ref[...] + lr[:, :RWKV_W])
    softplus = jnp.maximum(wz, 0.0) + jnp.log1p(jnp.exp(-jnp.abs(wz)))
    logw = -jnp.exp(-softplus - 0.5)
    a = jax.nn.sigmoid(a0_ref[...] + lr[:, RWKV_W:2 * RWKV_W])
    gate = lr[:, 2 * RWKV_W:]
    lo_full = _lo_mask((tb, LANES))
    kkf = kb * kk_ref[...]
    kb2 = kb * (1.0 + (a - 1.0) * ka_ref[...])
    bonus_src = r * kb2 * rk_ref[...]
    bonus = []
    for p in range(npair):
        sl = slice(p * LANES, (p + 1) * LANES)
        kf = kkf[:, sl]
        kn = kf * lax.rsqrt(jnp.maximum(_pair_sum(kf * kf, lo_full), 1e-24))
        kn_s[:, sl] = kn
        bv_s[:, sl] = kn * a[:, sl]
        bonus.append(_pair_sum(bonus_src[:, sl], lo_full) * v[:, sl])
    r_s[...] = r
    lw_s[...] = logw
    kb_s[...] = kb2
    v_s[...] = v

    c2 = 2 * chunk
    lo = _lo_mask((chunk, LANES))
    r2 = lax.broadcasted_iota(jnp.int32, (c2, c2), 0)
    q2 = lax.broadcasted_iota(jnp.int32, (c2, c2), 1)
    same = (r2 // chunk) == (q2 // chunk)
    strict = jnp.logical_and(same, (r2 % chunk) > (q2 % chunk))
    incl = jnp.logical_and(same, (r2 % chunk) >= (q2 % chunk))
    eye = (r2 == q2).astype(F32)
    tr = lax.broadcasted_iota(jnp.int32, (chunk, chunk), 0)
    tc = lax.broadcasted_iota(jnp.int32, (chunk, chunk), 1)
    tril = (tr >= tc).astype(F32)
    n_sq = int(math.log2(chunk)) - 1

    def stack(x):
        return jnp.concatenate([jnp.where(lo, x, 0.0), jnp.where(lo, 0.0, x)], axis=0)

    def chunk_step(ci, carry):
        rows = pl.ds(pl.multiple_of(ci * chunk, chunk), chunk)
        for p in range(npair):
            sl = slice(p * LANES, (p + 1) * LANES)
            lw = lw_s[rows, sl]
            cum = _dot(tril, lw, hp=True)
            pin = jnp.exp(cum)
            pex = jnp.exp(cum - lw)
            pinv = jnp.exp(-cum)
            pc = pin[chunk - 1:chunk, :]
            kn = kn_s[rows, sl]
            a_t = -kn * pex
            b_t = bv_s[rows, sl] * pinv
            k_t = kb_s[rows, sl] * pinv
            r_t = r_s[rows, sl] * pin
            vv = v_s[rows, sl]
            st = st_ref[p]

            ar = jnp.concatenate([stack(a_t), stack(r_t)], axis=0)
            gb_ = _dot_nt(ar, jnp.concatenate([b_t, b_t], axis=0), hp=True)
            gk_ = _dot_nt(ar, jnp.concatenate([k_t, k_t], axis=0), hp=True)
            l_ab = jnp.where(strict, gb_[:c2], 0.0)
            l_ak = jnp.where(strict, gk_[:c2], 0.0)
            l_rb = jnp.where(incl, gb_[c2:], 0.0)
            l_rk = jnp.where(incl, gk_[c2:], 0.0)

            tinv = eye + l_ab
            xp = _dot(l_ab, l_ab, hp=True)
            for _ in range(n_sq - 1):
                both = _dot(jnp.concatenate([tinv, xp], axis=0), xp, hp=True)
                tinv = tinv + both[:c2]
                xp = both[c2:]
            tinv = tinv + _dot(tinv, xp, hp=True)

            v_st = stack(vv)
            ars = _dot_nt(ar, st, hp=True)
            lv = _dot(jnp.concatenate([l_ak, l_rk], axis=0), v_st, hp=True)
            u_st = _dot(tinv, ars[:c2] + lv[:c2], hp=True)
            y_st = ars[c2:] + lv[c2:] + _dot(l_rb, u_st, hp=True)
            y_s[rows, sl] = y_st[:chunk] + y_st[chunk:]

            uv = jnp.concatenate([u_st, v_st], axis=0)
            bk = jnp.concatenate([stack(b_t * pc), stack(k_t * pc)], axis=0)
            st_ref[p] = st * pc + _dot_tn(uv, bk, hp=True)
        return carry

    lax.fori_loop(0, tb // chunk, chunk_step, 0)

    for p in range(npair):
        sl = slice(p * LANES, (p + 1) * LANES)
        y = y_s[:, sl]
        mu = _pair_sum(y, lo_full) * (1.0 / RWKV_N)
        d = y - mu
        var = _pair_sum(d * d, lo_full) * (1.0 / RWKV_N)
        yn = (d * lax.rsqrt(var + 64e-5)) * lnw_ref[:, sl] + lnb_ref[:, sl]
        o_ref[0, :, sl] = (yn + bonus[p]) * gate[:, sl]

    @pl.when(blk == pl.num_programs(1) - 1)
    def _():
        s_ref[0] = st_ref[...]


def _rwkv(pb, prev, s0_bd, mu, wlr, w0, a0, k_k, k_a, r_k, ln_w, ln_b, tb, chunk):
    bsz, seq, _ = pb.shape
    npair = RWKV_HEADS // 2
    vec = lambda n: pl.BlockSpec((1, n), lambda b, i: (0, 0))
    scr = lambda: pltpu.VMEM((tb, RWKV_W), F32)
    return pl.pallas_call(
        functools.partial(_rwkv_kernel, tb, chunk),
        grid=(bsz, seq // tb),
        in_specs=[
            pl.BlockSpec((1, tb, B_COLS), lambda b, i: (b, i, 0)),
            pl.BlockSpec((1, 1, B_COLS), lambda b, i: (b, 0, 0)),
            pl.BlockSpec((1, npair, LANES, LANES), lambda b, i: (b, 0, 0, 0)),
            vec(B_COLS),
            pl.BlockSpec(wlr.shape, lambda b, i: (0, 0)),
            vec(RWKV_W), vec(RWKV_W), vec(RWKV_W), vec(RWKV_W), vec(RWKV_W), vec(RWKV_W), vec(RWKV_W),
        ],
        out_specs=[
            pl.BlockSpec((1, tb, RWKV_W), lambda b, i: (b, i, 0)),
            pl.BlockSpec((1, npair, LANES, LANES), lambda b, i: (b, 0, 0, 0)),
        ],
        out_shape=[
            jax.ShapeDtypeStruct((bsz, seq, RWKV_W), F32),
            jax.ShapeDtypeStruct((bsz, npair, LANES, LANES), F32),
        ],
        scratch_shapes=[
            pltpu.VMEM((npair, LANES, LANES), F32),
            pltpu.VMEM((1, B_COLS), F32),
            scr(), scr(), scr(), scr(), scr(), scr(), scr(),
        ],
        compiler_params=_cparams(("parallel", "arbitrary")),
        name="rwkv7",
    )(pb, prev, s0_bd, mu.reshape(1, -1), wlr, w0.reshape(1, -1), a0.reshape(1, -1), k_k.reshape(1, -1),
      k_a.reshape(1, -1), r_k.reshape(1, -1), ln_w.reshape(1, -1), ln_b.reshape(1, -1))


def _sb_block(z, mask, carry, tri_u):
    log_beta = jnp.minimum(z, 0.0) - jnp.log1p(jnp.exp(-jnp.abs(z)))
    log_rem = jnp.where(mask, log_beta - z, 0.0)
    hi = log_rem.astype(BF16)
    lo = (log_rem - hi.astype(F32)).astype(BF16)
    after = (jnp.dot(hi, tri_u, preferred_element_type=F32) + jnp.dot(lo, tri_u, preferred_element_type=F32))
    w = jnp.where(mask, jnp.exp(log_beta + after + carry), 0.0)
    return w, carry + jnp.sum(log_rem, axis=-1, keepdims=True)


def _strict_upper(n):
    r = lax.broadcasted_iota(jnp.int32, (n, n), 0)
    c = lax.broadcasted_iota(jnp.int32, (n, n), 1)
    return (r > c).astype(BF16)


def _sb_prompt_kernel(tq, tk, q_ref, k_ref, v_ref, o_ref):
    qi = pl.program_id(2)
    q = q_ref[0] * (SB_DH ** -0.5)
    lo = _lo_mask((tq, LANES))
    qa = jnp.where(lo, q, 0.0).astype(BF16)
    qb = jnp.where(lo, 0.0, q).astype(BF16)
    tri_u = _strict_upper(tk)
    qpos = qi * tq + lax.broadcasted_iota(jnp.int32, (tq, tk), 0)
    kofs = lax.broadcasted_iota(jnp.int32, (tq, tk), 1)
    n_kb = (qi + 1) * (tq // tk)

    def body(j, state):
        acc_a, acc_b, car_a, car_b = state
        kb_idx = n_kb - 1 - j
        rows = pl.ds(pl.multiple_of(kb_idx * tk, tk), tk)
        kblk = k_ref[0, rows, :].astype(BF16)
        vblk = v_ref[0, rows, :].astype(BF16)
        mask = (kb_idx * tk + kofs) < qpos
        w_a, car_a = _sb_block(_dot_nt(qa, kblk), mask, car_a, tri_u)
        w_b, car_b = _sb_block(_dot_nt(qb, kblk), mask, car_b, tri_u)
        acc_a = acc_a + jnp.dot(w_a.astype(BF16), vblk, preferred_element_type=F32)
        acc_b = acc_b + jnp.dot(w_b.astype(BF16), vblk, preferred_element_type=F32)
        return acc_a, acc_b, car_a, car_b

    zero = jnp.zeros((tq, LANES), F32)
    zc = jnp.zeros((tq, 1), F32)
    acc_a, acc_b, _, _ = lax.fori_loop(0, n_kb, body, (zero, zero, zc, zc))
    o_ref[0] = jnp.where(lo, acc_a, acc_b)


def _sb_prompt(q, k, v, tq, tk):
    bsz, seq, _ = q.shape
    npair = SB_HEADS // 2
    return pl.pallas_call(
        functools.partial(_sb_prompt_kernel, tq, tk),
        grid=(bsz, npair, seq // tq),
        in_specs=[
            pl.BlockSpec((1, tq, LANES), lambda b, h, i: (b, i, h)),
            pl.BlockSpec((1, seq, LANES), lambda b, h, i: (b, 0, h)),
            pl.BlockSpec((1, seq, LANES), lambda b, h, i: (b, 0, h)),
        ],
        out_specs=pl.BlockSpec((1, tq, LANES), lambda b, h, i: (b, i, h)),
        out_shape=jax.ShapeDtypeStruct((bsz, seq, SB_W), F32),
        compiler_params=_cparams(("parallel", "parallel", "arbitrary")),
        name="sb_prompt",
    )(q, k, v)


def _sb_sample_kernel(npg, lq, *refs):
    pt_ref = refs[0]
    q_ref, kn_ref, vn_ref = refs[1:4]
    kp_refs = refs[4:4 + npg]
    vp_refs = refs[4 + npg:4 + 2 * npg]
    o_ref = refs[4 + 2 * npg]
    qbd_ref, acc_ref, car_ref = refs[5 + 2 * npg:]
    del pt_ref
    j = pl.program_id(1)
    rows = SB_HEADS * lq
    tri_u = _strict_upper(PAGE)

    def visit(kblk, vblk, mask):
        z = _dot_nt(qbd_ref[...], kblk)
        w, car = _sb_block(z, mask, car_ref[...], tri_u)
        car_ref[...] = car
        acc_ref[...] += jnp.dot(w.astype(BF16), vblk.astype(BF16), preferred_element_type=F32)

    @pl.when(j == 0)
    def _():
        q = q_ref[0] * (SB_DH ** -0.5)
        qt = jnp.concatenate([q] * SB_HEADS, axis=0)
        rh = lax.broadcasted_iota(jnp.int32, (rows, SB_W), 0) // lq
        ch = lax.broadcasted_iota(jnp.int32, (rows, SB_W), 1) // SB_DH
        qbd_ref[...] = jnp.where(rh == ch, qt, 0.0).astype(BF16)
        acc_ref[...] = jnp.zeros_like(acc_ref)
        car_ref[...] = jnp.zeros_like(car_ref)
        pad = jnp.zeros((PAGE - lq, SB_W), F32)
        kblk = jnp.concatenate([kn_ref[0], pad], axis=0)
        vblk = jnp.concatenate([vn_ref[0], pad], axis=0)
        qidx = lax.broadcasted_iota(jnp.int32, (rows, PAGE), 0) % lq
        kidx = lax.broadcasted_iota(jnp.int32, (rows, PAGE), 1)
        visit(kblk, vblk, kidx < qidx)

    full = jnp.full((rows, PAGE), True)
    for kp_ref, vp_ref in zip(kp_refs, vp_refs):
        visit(kp_ref[0, 0], vp_ref[0, 0], full)

    @pl.when(j == pl.num_programs(1) - 1)
    def _():
        acc = acc_ref[...]
        ch = lax.broadcasted_iota(jnp.int32, (lq, SB_W), 1) // SB_DH
        out = jnp.zeros((lq, SB_W), F32)
        for h in range(SB_HEADS):
            out = out + jnp.where(ch == h, acc[h * lq:(h + 1) * lq, :], 0.0)
        o_ref[0] = out


def _sb_sample(q, k_new, v_new, cache_k, cache_v, page_table, npg):
    bsz, lq, _ = q.shape
    n_pages = page_table.shape[1]
    rows = SB_HEADS * lq
    tok = pl.BlockSpec((1, lq, SB_W), lambda b, j, pt: (b, 0, 0))

    def page_spec(i):
        return pl.BlockSpec((1, 1, PAGE, SB_W), lambda b, j, pt: (0, pt[b, n_pages - 1 - (j * npg + i)], 0, 0))

    grid_spec = pltpu.PrefetchScalarGridSpec(
        num_scalar_prefetch=1,
        grid=(bsz, n_pages // npg),
        in_specs=[tok, tok, tok] + [page_spec(i) for i in range(npg)] * 2,
        out_specs=pl.BlockSpec((1, lq, SB_W), lambda b, j, pt: (b, 0, 0)),
        scratch_shapes=[
            pltpu.VMEM((rows, SB_W), BF16),
            pltpu.VMEM((rows, SB_W), F32),
            pltpu.VMEM((rows, 1), F32),
        ],
    )
    return pl.pallas_call(
        functools.partial(_sb_sample_kernel, npg, lq),
        grid_spec=grid_spec,
        out_shape=jax.ShapeDtypeStruct((bsz, lq, SB_W), F32),
        compiler_params=_cparams(("parallel", "arbitrary")),
        name="sb_sample",
    )(page_table, q, k_new, v_new, *([cache_k] * npg), *([cache_v] * npg))


def _first_index(cond, idx, big):
    return jnp.min(jnp.where(cond, idx, big), axis=(0, 1), keepdims=True)


def _router_kernel(x_ref, g_ref, sh_ref, sc_ref, rw_ref, rb_ref, h_ref, comb_ref):
    h = _modulated_norm(x_ref[0], g_ref[...], sh_ref[0], sc_ref[0])
    h_ref[0] = h.astype(BF16)
    tl = h.shape[0]
    logits = _dot_nt(rw_ref[...], h, hp=True)
    scores = jax.nn.sigmoid(logits).reshape(N_GROUPS, GROUP_SIZE, tl)
    biased = scores + rb_ref[...].reshape(N_GROUPS, GROUP_SIZE, 1)
    neg = -jnp.inf

    jidx = lax.broadcasted_iota(jnp.int32, biased.shape, 1)
    m1 = jnp.max(biased, axis=1, keepdims=True)
    first = jnp.min(jnp.where(biased == m1, jidx, GROUP_SIZE), axis=1, keepdims=True)
    m2 = jnp.max(jnp.where(jidx == first, neg, biased), axis=1, keepdims=True)
    gscore = m1 + m2

    gidx = lax.broadcasted_iota(jnp.int32, gscore.shape, 0)
    gsel = jnp.zeros(gscore.shape, jnp.bool_)
    for _ in range(TOPK_GROUPS):
        m = jnp.max(gscore, axis=0, keepdims=True)
        pick = gidx == jnp.min(jnp.where(gscore == m, gidx, N_GROUPS), axis=0, keepdims=True)
        gsel = jnp.logical_or(gsel, pick)
        gscore = jnp.where(pick, neg, gscore)

    cand = jnp.where(gsel, biased, neg)
    eidx = lax.broadcasted_iota(jnp.int32, cand.shape, 0) * GROUP_SIZE + jidx
    esel = jnp.zeros(cand.shape, jnp.bool_)
    for _ in range(TOP_K):
        m = jnp.max(cand, axis=(0, 1), keepdims=True)
        pick = eidx == _first_index(cand == m, eidx, N_EXPERTS)
        esel = jnp.logical_or(esel, pick)
        cand = jnp.where(pick, neg, cand)

    sel = jnp.where(esel, scores, 0.0)
    wts = sel / jnp.sum(sel, axis=(0, 1), keepdims=True) * ROUTED_SCALE
    comb_ref[0] = wts.reshape(N_EXPERTS, tl).T


def _router(x3, g, sh3, sc3, router_w, router_bias, tl):
    gsz, lg, _ = x3.shape
    lm = sh3.shape[1]
    tm = 1 if lm == 1 else tl
    mod_map = (lambda b, i: (b, 0, 0)) if lm == 1 else (lambda b, i: (b, i, 0))
    return pl.pallas_call(
        _router_kernel,
        grid=(gsz, lg // tl),
        in_specs=[
            pl.BlockSpec((1, tl, D_MODEL), lambda b, i: (b, i, 0)),
            pl.BlockSpec((1, D_MODEL), lambda b, i: (0, 0)),
            pl.BlockSpec((1, tm, D_MODEL), mod_map),
            pl.BlockSpec((1, tm, D_MODEL), mod_map),
            pl.BlockSpec((N_EXPERTS, D_MODEL), lambda b, i: (0, 0)),
            pl.BlockSpec((N_EXPERTS, 1), lambda b, i: (0, 0)),
        ],
        out_specs=[
            pl.BlockSpec((1, tl, D_MODEL), lambda b, i: (b, i, 0)),
            pl.BlockSpec((1, tl, N_EXPERTS), lambda b, i: (b, i, 0)),
        ],
        out_shape=[
            jax.ShapeDtypeStruct((gsz, lg, D_MODEL), BF16),
            jax.ShapeDtypeStruct((gsz, lg, N_EXPERTS), F32),
        ],
        compiler_params=_cparams(("parallel", "parallel")),
        name="moe_router",
    )(x3, g.reshape(1, D_MODEL), sh3, sc3, router_w.T, router_bias.reshape(N_EXPERTS, 1))


def _swiglu_hidden(h, wg, wu):
    return _silu(jnp.dot(h, wg, preferred_element_type=F32)) * jnp.dot(h, wu, preferred_element_type=F32)


def _moe_kernel(final, h_ref, comb_ref, wg_ref, wu_ref, wd_ref, sg_ref, su_ref, sd_ref, x_ref, gate_ref, fn_ref,
                o_ref, acc_ref):
    e = pl.program_id(2)
    h = h_ref[0]

    @pl.when(e == 0)
    def _():
        act = _swiglu_hidden(h, sg_ref[...], su_ref[...])
        acc_ref[...] = jnp.dot(act.astype(BF16), sd_ref[...], preferred_element_type=F32)

    comb = comb_ref[0]
    lane = lax.broadcasted_iota(jnp.int32, comb.shape, 1)
    c = jnp.sum(jnp.where(lane == e, comb, 0.0), axis=-1, keepdims=True)
    act = _swiglu_hidden(h, wg_ref[0], wu_ref[0]) * c
    acc_ref[...] += jnp.dot(act.astype(BF16), wd_ref[0], preferred_element_type=F32)

    @pl.when(e == pl.num_programs(2) - 1)
    def _():
        y = x_ref[0] + gate_ref[0] * acc_ref[...]
        if final:
            y = (y * lax.rsqrt(jnp.mean(y * y, axis=-1, keepdims=True) + EPS)) * fn_ref[...]
        o_ref[0] = y


def _moe(x3, gate3, h3, comb3, wg, wu, wd, sg, su, sd, final_norm, tl, final):
    gsz, lg, _ = x3.shape
    lm = gate3.shape[1]
    tm = 1 if lm == 1 else tl
    mod_map = (lambda b, i, e: (b, 0, 0)) if lm == 1 else (lambda b, i, e: (b, i, 0))
    tok = lambda n: pl.BlockSpec((1, tl, n), lambda b, i, e: (b, i, 0))
    full = lambda a: pl.BlockSpec(a.shape, lambda b, i, e: (0, 0))
    return pl.pallas_call(
        functools.partial(_moe_kernel, final),
        grid=(gsz, lg // tl, N_EXPERTS),
        in_specs=[
            tok(D_MODEL), tok(N_EXPERTS),
            pl.BlockSpec((1, D_MODEL, HIDDEN), lambda b, i, e: (e, 0, 0)),
            pl.BlockSpec((1, D_MODEL, HIDDEN), lambda b, i, e: (e, 0, 0)),
            pl.BlockSpec((1, HIDDEN, D_MODEL), lambda b, i, e: (e, 0, 0)),
            full(sg), full(su), full(sd),
            tok(D_MODEL),
            pl.BlockSpec((1, tm, D_MODEL), mod_map),
            pl.BlockSpec((1, D_MODEL), lambda b, i, e: (0, 0)),
        ],
        out_specs=tok(D_MODEL),
        out_shape=jax.ShapeDtypeStruct(x3.shape, F32),
        scratch_shapes=[pltpu.VMEM((tl, D_MODEL), F32)],
        compiler_params=_cparams(("parallel", "parallel", "arbitrary")),
        name="moe_experts",
    )(h3, comb3, wg, wu, wd, sg, su, sd, x3, gate3, final_norm.reshape(1, D_MODEL))


def _block_diag_pairs(s):
    bsz, nh, n, _ = s.shape
    s = s.reshape(bsz, nh // 2, 2, n, n)
    z = jnp.zeros_like(s[:, :, 0])
    top = jnp.concatenate([s[:, :, 0], z], axis=-1)
    bot = jnp.concatenate([z, s[:, :, 1]], axis=-1)
    return jnp.concatenate([top, bot], axis=-2)


def _unpair(s_bd):
    bsz, npair, n2, _ = s_bd.shape
    n = n2 // 2
    return jnp.stack([s_bd[:, :, :n, :n], s_bd[:, :, n:, n:]], axis=2).reshape(bsz, npair * 2, n, n)


def _rotary_tables(pos):
    half = RET_DK // 2
    inv = ROPE_BASE ** (-jnp.arange(half, dtype=F32) / half)
    ang = pos.astype(F32)[:, None] * inv[None, :]
    cos, sin = jnp.cos(ang), jnp.sin(ang)
    cos_t = jnp.concatenate([cos, cos, cos, cos], axis=-1)
    sin_t = jnp.concatenate([-sin, sin, -sin, sin], axis=-1)
    return cos_t, sin_t


def _trunk(x, c, pos, ret_s, rwkv_s, shift_s, cache, p, flat):
    bsz, seq, _ = x.shape
    mod = _ada(c, p['ada_w'], p['ada_b'])
    mod = mod.reshape(DEPTH, bsz, 6, D_MODEL)
    if flat:
        tok = lambda t: t.reshape(1, bsz * seq, t.shape[-1])
        untok = lambda t: t.reshape(bsz, seq, t.shape[-1])
        modv = lambda l, j: jnp.broadcast_to(mod[l, :, j][:, None, :], (bsz, seq, D_MODEL)).reshape(1, bsz * seq, D_MODEL)
        tl_proj = tl_moe = bsz * seq
    else:
        tok = untok = lambda t: t
        modv = lambda l, j: mod[l, :, j][:, None, :]
        tl_proj, tl_moe = min(256, seq), min(1024, seq)

    x3 = tok(x)
    w_in = p['w_in_ab'][0].astype(BF16)
    a_cols = 4 * RET_W
    zeros_a = jnp.zeros((1, a_cols), F32)
    zeros_b = jnp.zeros((1, B_COLS), F32)
    pa, pb = _norm_proj(x3, p['norm_mix'][0], modv(0, 0), modv(0, 1), [w_in[:, :a_cols], w_in[:, a_cols:]],
                        [zeros_a, zeros_b], tl_proj)
    pa, pb = untok(pa), untok(pb)
    cos_t, sin_t = _rotary_tables(pos)
    lgs = jnp.log1p(-jnp.exp2(-5.0 - jnp.arange(RET_HEADS, dtype=F32)))
    ret_chunk = math.gcd(seq, 128)
    o_a, ret_new = _retention(pa, cos_t, sin_t, lgs, p['ret_gn_w'][0], p['ret_gn_b'][0],
                              _block_diag_pairs(ret_s), ret_chunk)
    wlr = jnp.zeros((LOWRANK, 3 * RWKV_W), F32)
    wlr = wlr.at[0:64, 0:RWKV_W].set(p['rwkv_w_up'][0])
    wlr = wlr.at[64:128, RWKV_W:2 * RWKV_W].set(p['rwkv_a_up'][0])
    wlr = wlr.at[128:256, 2 * RWKV_W:].set(p['rwkv_g_up'][0])
    tb = min(seq, 256)
    o_b, rwkv_new = _rwkv(pb, shift_s[:, None, :], _block_diag_pairs(rwkv_s), p['rwkv_mu'][0], wlr.astype(BF16),
                          p['rwkv_w0'][0], p['rwkv_a0'][0], p['rwkv_k_k'][0], p['rwkv_k_a'][0],
                          p['rwkv_r_k'][0].reshape(-1), p['rwkv_ln_w'][0], p['rwkv_ln_b'][0],
                          tb, min(RWKV_CHUNK, seq))
    w_out = p['w_out_ab'][0].astype(BF16)
    x3 = _out_proj(x3, modv(0, 2), [tok(o_a), tok(o_b)], [w_out[:RET_W], w_out[RET_W:]], tl_proj)
    x3 = _moe_layer(x3, 0, modv, p, tl_moe, final=False)

    wq, wk, wv = jnp.split(p['w_qkv_c'][0].astype(BF16), 3, axis=1)
    q, k, v = _norm_proj(x3, p['norm_mix'][1], modv(1, 0), modv(1, 1), [wq, wk, wv],
                         [p['b_q_c'][0][None, :], p['b_k_c'][0][None, :], jnp.zeros((1, SB_W), F32)], tl_proj)
    q, k, v = untok(q), untok(k), untok(v)
    if cache is None:
        o = _sb_prompt(q, k, v, min(256, seq), 128)
    else:
        cache_k, cache_v, page_table = cache
        o = _sb_sample(q, k, v, cache_k, cache_v, page_table, 4)
    x3 = _out_proj(x3, modv(1, 2), [tok(o)], [p['w_out_c'][0].astype(BF16)], tl_proj)
    y3 = _moe_layer(x3, 1, modv, p, tl_moe, final=True)

    kv_shape = (1, bsz, seq, SB_HEADS, SB_DH)
    return (untok(y3), _unpair(ret_new)[None], _unpair(rwkv_new)[None], pb[:, -1][None],
            k.reshape(kv_shape), v.reshape(kv_shape))


def _moe_layer(x3, l, modv, p, tl, final):
    tl_r = min(tl, 512)
    h3, comb3 = _router(x3, p['norm_ffn'][l], modv(l, 3), modv(l, 4), p['router_w'][l], p['router_bias'][l], tl_r)
    return _moe(x3, modv(l, 5), h3, comb3, p['exp_w_gate'][l].astype(BF16), p['exp_w_up'][l].astype(BF16),
                p['exp_w_down'][l].astype(BF16), p['sh_w_gate'][l].astype(BF16), p['sh_w_up'][l].astype(BF16),
                p['sh_w_down'][l].astype(BF16), p['final_norm'], tl, final)


def kernel(x_prompt, x_sample, c_prompt, c_sample, state_ret, state_rwkv, state_shift, cache_k, cache_v, page_table, ada_w, ada_b, norm_mix, norm_ffn, final_norm, w_in_ab, w_out_ab, ret_gn_w, ret_gn_b, rwkv_mu, rwkv_w0, rwkv_w_up, rwkv_a0, rwkv_a_up, rwkv_g_up, rwkv_k_k, rwkv_k_a, rwkv_r_k, rwkv_ln_w, rwkv_ln_b, w_qkv_c, b_q_c, b_k_c, w_out_c, router_w, router_bias, exp_w_gate, exp_w_up, exp_w_down, sh_w_gate, sh_w_up, sh_w_down):
    p = dict(ada_w=ada_w, ada_b=ada_b, norm_mix=norm_mix, norm_ffn=norm_ffn, final_norm=final_norm,
             w_in_ab=w_in_ab, w_out_ab=w_out_ab, ret_gn_w=ret_gn_w, ret_gn_b=ret_gn_b, rwkv_mu=rwkv_mu,
             rwkv_w0=rwkv_w0, rwkv_w_up=rwkv_w_up, rwkv_a0=rwkv_a0, rwkv_a_up=rwkv_a_up, rwkv_g_up=rwkv_g_up,
             rwkv_k_k=rwkv_k_k, rwkv_k_a=rwkv_k_a, rwkv_r_k=rwkv_r_k, rwkv_ln_w=rwkv_ln_w, rwkv_ln_b=rwkv_ln_b,
             w_qkv_c=w_qkv_c, b_q_c=b_q_c, b_k_c=b_k_c, w_out_c=w_out_c, router_w=router_w,
             router_bias=router_bias, exp_w_gate=exp_w_gate, exp_w_up=exp_w_up, exp_w_down=exp_w_down,
             sh_w_gate=sh_w_gate, sh_w_up=sh_w_up, sh_w_down=sh_w_down)
    bp, lp, _ = x_prompt.shape
    bs, ls, _ = x_sample.shape
    zeros = lambda *s: jnp.zeros(s, F32)
    y_p, ret_p, rwkv_p, shift_p, k_p, v_p = _trunk(
        x_prompt, c_prompt, jnp.arange(lp), zeros(bp, RET_HEADS, RET_DK, RET_DK),
        zeros(bp, RWKV_HEADS, RWKV_N, RWKV_N), zeros(bp, B_COLS), None, p, flat=False)
    n_pages = page_table.shape[1]
    n_pool = cache_k.shape[1]
    cache = (cache_k.reshape(1, n_pool, PAGE, SB_W), cache_v.reshape(1, n_pool, PAGE, SB_W), page_table)
    y_s, ret_s, rwkv_s, shift_s, k_s, v_s = _trunk(
        x_sample, c_sample, n_pages * PAGE + jnp.arange(ls), state_ret[0], state_rwkv[0], state_shift[0],
        cache, p, flat=True)
    return (y_p, y_s, ret_p, ret_s, rwkv_p, rwkv_s, shift_p, shift_s, k_p, v_p, k_s, v_s)
```

```python
import functools
import math

import jax
import jax.numpy as jnp
from jax import lax
from jax.experimental import pallas as pl
from jax.experimental.pallas import tpu as pltpu

F32 = jnp.float32
BF16 = jnp.bfloat16
HIGHEST = lax.Precision.HIGHEST

D_MODEL = 1024
DEPTH = 2
PAGE = 128
RET_HEADS = 8
RET_DK = 64
RWKV_HEADS = 8
RWKV_N = 64
RWKV_W = RWKV_HEADS * RWKV_N
RET_W = RET_HEADS * RET_DK
LOWRANK = 256
B_COLS = 3 * RWKV_W + LOWRANK
SB_HEADS = 16
SB_DH = 64
SB_W = SB_HEADS * SB_DH
N_EXPERTS = 64
N_GROUPS = 8
GROUP_SIZE = N_EXPERTS // N_GROUPS
TOPK_GROUPS = 4
TOP_K = 8
HIDDEN = 256
ROUTED_SCALE = 2.5
EPS = 1e-6
ROPE_BASE = 10000.0
LANES = 128
VMEM_LIMIT = 56 * 1024 * 1024

MOE_TB = 256
MOE_UNIT = 16
MOE_RT = 512
MOE_CH = 512
MOE_RMAX = -(-(MOE_TB * TOP_K + N_EXPERTS * (MOE_UNIT - 1)) // MOE_CH) * MOE_CH
MOE_NU = MOE_RMAX // MOE_UNIT
MOE_INFO_BITS = 9
MOE_INFO_SHIFT = 1 << MOE_INFO_BITS

RWKV_CHUNK = 16
RWKV_HP_LOCAL = False
RWKV_HP_STATE = False


def _cparams(sem):
    return pltpu.CompilerParams(dimension_semantics=sem, vmem_limit_bytes=VMEM_LIMIT)


def _dot(a, b, hp=False):
    if hp:
        return jnp.dot(a.astype(F32), b.astype(F32), preferred_element_type=F32, precision=HIGHEST)
    return jnp.dot(a.astype(BF16), b.astype(BF16), preferred_element_type=F32)


def _dot_nt(a, b, hp=False):
    dn = (((1,), (1,)), ((), ()))
    if hp:
        return lax.dot_general(a.astype(F32), b.astype(F32), dn, preferred_element_type=F32, precision=HIGHEST)
    return lax.dot_general(a.astype(BF16), b.astype(BF16), dn, preferred_element_type=F32)


def _dot_tn(a, b, hp=False):
    dn = (((0,), (0,)), ((), ()))
    if hp:
        return lax.dot_general(a.astype(F32), b.astype(F32), dn, preferred_element_type=F32, precision=HIGHEST)
    return lax.dot_general(a.astype(BF16), b.astype(BF16), dn, preferred_element_type=F32)


def _silu(x):
    return x * jax.nn.sigmoid(x)


def _lo_mask(shape):
    return (lax.broadcasted_iota(jnp.int32, shape, len(shape) - 1) % LANES) < (LANES // 2)


def _pair_sum(x, lo):
    s_lo = jnp.sum(jnp.where(lo, x, 0.0), axis=-1, keepdims=True)
    s_hi = jnp.sum(jnp.where(lo, 0.0, x), axis=-1, keepdims=True)
    return jnp.where(lo, s_lo, s_hi)


def _ada_kernel(c_ref, w_ref, b_ref, o_ref):
    o_ref[0] = _dot(_silu(c_ref[...]), w_ref[0]) + b_ref[0]


def _ada(c, ada_w, ada_b):
    bsz = c.shape[0]
    n = ada_w.shape[-1]
    tn = D_MODEL
    return pl.pallas_call(
        _ada_kernel,
        grid=(DEPTH, n // tn),
        in_specs=[
            pl.BlockSpec((bsz, D_MODEL), lambda l, j: (0, 0)),
            pl.BlockSpec((1, D_MODEL, tn), lambda l, j: (l, 0, j)),
            pl.BlockSpec((1, 1, tn), lambda l, j: (l, 0, j)),
        ],
        out_specs=pl.BlockSpec((1, bsz, tn), lambda l, j: (l, 0, j)),
        out_shape=jax.ShapeDtypeStruct((DEPTH, bsz, n), F32),
        compiler_params=_cparams(("parallel", "parallel")),
        name="ada_mod",
    )(c, ada_w, ada_b.reshape(DEPTH, 1, n))


def _modulated_norm(x, g, sh, sc):
    y = x * lax.rsqrt(jnp.mean(x * x, axis=-1, keepdims=True) + EPS)
    return (y * g) * (1.0 + sc) + sh


def _norm_proj_kernel(n_out, x_ref, g_ref, sh_ref, sc_ref, *refs):
    w_refs, b_refs, o_refs = refs[:n_out], refs[n_out:2 * n_out], refs[2 * n_out:]
    h = _modulated_norm(x_ref[0], g_ref[...], sh_ref[0], sc_ref[0]).astype(BF16)
    for w_ref, b_ref, o_ref in zip(w_refs, b_refs, o_refs):
        o_ref[0] = jnp.dot(h, w_ref[...], preferred_element_type=F32) + b_ref[...]


def _norm_proj(x3, g, sh3, sc3, ws, bs, tl):
    gsz, lg, _ = x3.shape
    lm = sh3.shape[1]
    tm = 1 if lm == 1 else tl
    mod_map = (lambda b, i: (b, 0, 0)) if lm == 1 else (lambda b, i: (b, i, 0))
    n_out = len(ws)
    in_specs = [
        pl.BlockSpec((1, tl, D_MODEL), lambda b, i: (b, i, 0)),
        pl.BlockSpec((1, D_MODEL), lambda b, i: (0, 0)),
        pl.BlockSpec((1, tm, D_MODEL), mod_map),
        pl.BlockSpec((1, tm, D_MODEL), mod_map),
    ]
    in_specs += [pl.BlockSpec(w.shape, lambda b, i: (0, 0)) for w in ws]
    in_specs += [pl.BlockSpec(b.shape, lambda b, i: (0, 0)) for b in bs]
    return pl.pallas_call(
        functools.partial(_norm_proj_kernel, n_out),
        grid=(gsz, lg // tl),
        in_specs=in_specs,
        out_specs=[pl.BlockSpec((1, tl, w.shape[1]), lambda b, i: (b, i, 0)) for w in ws],
        out_shape=[jax.ShapeDtypeStruct((gsz, lg, w.shape[1]), F32) for w in ws],
        compiler_params=_cparams(("parallel", "parallel")),
        name="norm_proj",
    )(x3, g.reshape(1, D_MODEL), sh3, sc3, *ws, *bs)


def _out_proj_kernel(n_in, x_ref, gate_ref, *refs):
    o_refs, w_refs, y_ref = refs[:n_in], refs[n_in:2 * n_in], refs[2 * n_in]
    acc = None
    for o_ref, w_ref in zip(o_refs, w_refs):
        t = jnp.dot(o_ref[0].astype(BF16), w_ref[...], preferred_element_type=F32)
        acc = t if acc is None else acc + t
    y_ref[0] = x_ref[0] + gate_ref[0] * acc


def _out_proj(x3, gate3, os_, ws, tl):
    gsz, lg, _ = x3.shape
    lm = gate3.shape[1]
    tm = 1 if lm == 1 else tl
    mod_map = (lambda b, i: (b, 0, 0)) if lm == 1 else (lambda b, i: (b, i, 0))
    n_in = len(os_)
    in_specs = [
        pl.BlockSpec((1, tl, D_MODEL), lambda b, i: (b, i, 0)),
        pl.BlockSpec((1, tm, D_MODEL), mod_map),
    ]
    in_specs += [pl.BlockSpec((1, tl, o.shape[-1]), lambda b, i: (b, i, 0)) for o in os_]
    in_specs += [pl.BlockSpec(w.shape, lambda b, i: (0, 0)) for w in ws]
    return pl.pallas_call(
        functools.partial(_out_proj_kernel, n_in),
        grid=(gsz, lg // tl),
        in_specs=in_specs,
        out_specs=pl.BlockSpec((1, tl, D_MODEL), lambda b, i: (b, i, 0)),
        out_shape=jax.ShapeDtypeStruct(x3.shape, F32),
        compiler_params=_cparams(("parallel", "parallel")),
        name="out_proj",
    )(x3, gate3, *os_, *ws)


def _rot_half(x, lo32):
    return jnp.where(lo32, pltpu.roll(x, LANES - 32, 1), pltpu.roll(x, 32, 1))


def _retention_kernel(chunk, lgs_ref, q_ref, k_ref, v_ref, g_ref, cos_ref, sin_ref, gw_ref, gb_ref, s0_ref,
                      o_ref, s_ref, st_ref):
    hp = pl.program_id(1)
    c = pl.program_id(2)

    @pl.when(c == 0)
    def _():
        st_ref[...] = s0_ref[0, 0]

    lane = lax.broadcasted_iota(jnp.int32, (chunk, LANES), 1)
    lo = lane < (LANES // 2)
    lo32 = (lane % RET_DK) < (RET_DK // 2)
    lg_a = lgs_ref[2 * hp]
    lg_b = lgs_ref[2 * hp + 1]
    lg = jnp.where(lo, lg_a, lg_b)

    cos = cos_ref[...]
    sin = sin_ref[...]
    q = q_ref[0]
    k = k_ref[0]
    q = q * cos + _rot_half(q, lo32) * sin
    k = (k * cos + _rot_half(k, lo32) * sin) * (RET_DK ** -0.5)
    v = v_ref[0]

    ti = lax.broadcasted_iota(jnp.int32, (chunk, LANES), 0).astype(F32)
    q_dec = jnp.exp(lg * (ti + 1.0))
    k_dec = jnp.exp(lg * (chunk - 1.0 - ti))

    ii = lax.broadcasted_iota(jnp.int32, (chunk, chunk), 0)
    jj = lax.broadcasted_iota(jnp.int32, (chunk, chunk), 1)
    diff = jnp.maximum(ii - jj, 0).astype(F32)
    causal = ii >= jj
    kb = k.astype(BF16)
    vb = v.astype(BF16)

    def head(lg_h, sel):
        dmask = jnp.where(causal, jnp.exp(lg_h * diff), 0.0)
        qh = jnp.where(sel, q, 0.0)
        sc = _dot_nt(qh, kb) * dmask
        return _dot(sc, vb)

    o = jnp.where(lo, head(lg_a, lo), head(lg_b, jnp.logical_not(lo)))
    st = st_ref[...]
    o = o + _dot(q * q_dec, st)

    ri = lax.broadcasted_iota(jnp.int32, (LANES, LANES), 0)
    ci = lax.broadcasted_iota(jnp.int32, (LANES, LANES), 1)
    same_head = (ri < RET_DK) == (ci < RET_DK)
    c_dec = jnp.exp(jnp.where(ri < RET_DK, lg_a, lg_b) * float(chunk))
    st_new = st * c_dec + jnp.where(same_head, _dot_tn(k * k_dec, vb), 0.0)
    st_ref[...] = st_new

    mu = _pair_sum(o, lo) * (1.0 / RET_DK)
    d = o - mu
    var = _pair_sum(d * d, lo) * (1.0 / RET_DK)
    y = (d * lax.rsqrt(var + 1e-5)) * gw_ref[...] + gb_ref[...]
    o_ref[0] = y * _silu(g_ref[0])

    @pl.when(c == pl.num_programs(2) - 1)
    def _():
        s_ref[0, 0] = st_new


def _retention(pa, cos_t, sin_t, lgs, gn_w, gn_b, s0_bd, chunk):
    bsz, seq, _ = pa.shape
    npair = RET_HEADS // 2
    nc = seq // chunk
    col = lambda off: (lambda b, h, c: (b, c, off + h))
    return pl.pallas_call(
        functools.partial(_retention_kernel, chunk),
        grid=(bsz, npair, nc),
        in_specs=[
            pl.BlockSpec(memory_space=pltpu.SMEM),
            pl.BlockSpec((1, chunk, LANES), col(0)),
            pl.BlockSpec((1, chunk, LANES), col(npair)),
            pl.BlockSpec((1, chunk, LANES), col(2 * npair)),
            pl.BlockSpec((1, chunk, LANES), col(3 * npair)),
            pl.BlockSpec((chunk, LANES), lambda b, h, c: (c, 0)),
            pl.BlockSpec((chunk, LANES), lambda b, h, c: (c, 0)),
            pl.BlockSpec((1, LANES), lambda b, h, c: (0, h)),
            pl.BlockSpec((1, LANES), lambda b, h, c: (0, h)),
            pl.BlockSpec((1, 1, LANES, LANES), lambda b, h, c: (b, h, 0, 0)),
        ],
        out_specs=[
            pl.BlockSpec((1, chunk, LANES), lambda b, h, c: (b, c, h)),
            pl.BlockSpec((1, 1, LANES, LANES), lambda b, h, c: (b, h, 0, 0)),
        ],
        out_shape=[
            jax.ShapeDtypeStruct((bsz, seq, RET_W), F32),
            jax.ShapeDtypeStruct((bsz, npair, LANES, LANES), F32),
        ],
        scratch_shapes=[pltpu.VMEM((LANES, LANES), F32)],
        compiler_params=_cparams(("parallel", "parallel", "arbitrary")),
        name="retention",
    )(lgs, pa, pa, pa, pa, cos_t, sin_t, gn_w.reshape(1, RET_W), gn_b.reshape(1, RET_W), s0_bd)


def _rwkv_kernel(tb, chunk, pb_ref, prev_ref, s0_ref, mu_ref, wlr_ref, w0_ref, a0_ref, kk_ref, ka_ref, rk_ref,
                 lnw_ref, lnb_ref, o_ref, s_ref,
                 st0_ref, st1_ref, st2_ref, st3_ref, carry_ref, a_s, b_s, k_s, r_s, v_s, bp_s, kp_s, pc_s, y_s,
                 tinv_s, lrb_s, lva_s, lvr_s):
    npair = RWKV_HEADS // 2
    st_refs = (st0_ref, st1_ref, st2_ref, st3_ref)
    blk = pl.program_id(1)

    @pl.when(blk == 0)
    def _():
        for p in range(npair):
            st_refs[p][...] = s0_ref[0, p]
        carry_ref[...] = prev_ref[0]

    pb = pb_ref[0]
    row = lax.broadcasted_iota(jnp.int32, pb.shape, 0)
    prev = jnp.where(row == 0, carry_ref[...], pltpu.roll(pb, 1, 0))
    carry_ref[...] = pb[tb - 1:tb, :]
    pbs = pb + (prev - pb) * mu_ref[...]
    r = pbs[:, :RWKV_W]
    kb = pbs[:, RWKV_W:2 * RWKV_W]
    v = pbs[:, 2 * RWKV_W:3 * RWKV_W]
    tail = pbs[:, 3 * RWKV_W:]
    tl_lane = lax.broadcasted_iota(jnp.int32, tail.shape, 1)
    act = jnp.where(tl_lane < 64, jnp.tanh(tail), jnp.where(tl_lane < 128, tail, jax.nn.sigmoid(tail)))
    lr = _dot(act, wlr_ref[...])
    wz = -(w0_ref[...] + lr[:, :RWKV_W])
    softplus = jnp.maximum(wz, 0.0) + jnp.log1p(jnp.exp(-jnp.abs(wz)))
    logw = -jnp.exp(-softplus - 0.5)
    a = jax.nn.sigmoid(a0_ref[...] + lr[:, RWKV_W:2 * RWKV_W])
    gate = lr[:, 2 * RWKV_W:]
    lo_full = _lo_mask((tb, LANES))
    kkf = kb * kk_ref[...]
    kb2 = kb * (1.0 + (a - 1.0) * ka_ref[...])
    bonus_src = r * kb2 * rk_ref[...]
    ti = lax.broadcasted_iota(jnp.int32, (tb, tb), 0)
    tj = lax.broadcasted_iota(jnp.int32, (tb, tb), 1)
    same_chunk = (ti // chunk) == (tj // chunk)
    sel = jnp.concatenate([jnp.logical_and(same_chunk, ti >= tj), same_chunk], axis=0).astype(F32)
    cums = _dot(sel, logw, hp=True)
    cum, tot = cums[:tb], cums[tb:]
    pin = jnp.exp(cum)
    pinv = jnp.exp(-cum)
    prem = jnp.exp(tot - cum)
    bonus = []
    for p in range(npair):
        sl = slice(p * LANES, (p + 1) * LANES)
        kf = kkf[:, sl]
        kn = kf * lax.rsqrt(jnp.maximum(_pair_sum(kf * kf, lo_full), 1e-24))
        bv = kn * a[:, sl]
        a_s[:, sl] = -kn * jnp.exp(cum[:, sl] - logw[:, sl])
        b_s[:, sl] = bv * pinv[:, sl]
        bp_s[:, sl] = bv * prem[:, sl]
        bonus.append(_pair_sum(bonus_src[:, sl], lo_full) * v[:, sl])
    k_s[...] = kb2 * pinv
    kp_s[...] = kb2 * prem
    r_s[...] = r * pin
    v_s[...] = v
    pc_s[...] = jnp.exp(tot)

    c2 = 2 * chunk
    lo = _lo_mask((chunk, LANES))
    r2 = lax.broadcasted_iota(jnp.int32, (c2, c2), 0)
    q2 = lax.broadcasted_iota(jnp.int32, (c2, c2), 1)
    same = (r2 // chunk) == (q2 // chunk)
    strict = jnp.logical_and(same, (r2 % chunk) > (q2 % chunk))
    incl = jnp.logical_and(same, (r2 % chunk) >= (q2 % chunk))
    eye = (r2 == q2).astype(F32)
    n_sq = int(math.log2(chunk)) - 1
    pairs = range(npair)
    lanes = [slice(p * LANES, (p + 1) * LANES) for p in pairs]

    def stack(x):
        return jnp.concatenate([jnp.where(lo, x, 0.0), jnp.where(lo, 0.0, x)], axis=0)

    def chunk_rows(ci):
        return pl.ds(pl.multiple_of(ci * chunk, chunk), chunk)

    def local_step(ci, carry):
        rows = chunk_rows(ci)
        ar = [jnp.concatenate([stack(a_s[rows, s]), stack(r_s[rows, s])], axis=0) for s in lanes]
        b2 = [b_s[rows, s] for s in lanes]
        k2 = [k_s[rows, s] for s in lanes]
        gb_ = [_dot_nt(ar[p], jnp.concatenate([b2[p], b2[p]], axis=0), hp=RWKV_HP_LOCAL) for p in pairs]
        gk_ = [_dot_nt(ar[p], jnp.concatenate([k2[p], k2[p]], axis=0), hp=RWKV_HP_LOCAL) for p in pairs]
        l_ab = [jnp.where(strict, gb_[p][:c2], 0.0) for p in pairs]
        tinv = [eye + l_ab[p] for p in pairs]
        xp = [_dot(l_ab[p], l_ab[p], hp=True) for p in pairs]
        for _ in range(n_sq - 1):
            both = [_dot(jnp.concatenate([tinv[p], xp[p]], axis=0), xp[p], hp=True) for p in pairs]
            tinv = [tinv[p] + both[p][:c2] for p in pairs]
            xp = [both[p][c2:] for p in pairs]
        tinv = [tinv[p] + _dot(tinv[p], xp[p], hp=True) for p in pairs]
        for p in pairs:
            l_akrk = jnp.concatenate([jnp.where(strict, gk_[p][:c2], 0.0), jnp.where(incl, gk_[p][c2:], 0.0)], axis=0)
            lv = _dot(l_akrk, stack(v_s[rows, lanes[p]]), hp=RWKV_HP_LOCAL)
            tinv_s[ci, p] = tinv[p]
            lrb_s[ci, p] = jnp.where(incl, gb_[p][c2:], 0.0)
            lva_s[ci, p] = lv[:c2]
            lvr_s[ci, p] = lv[c2:]
        return carry

    lax.fori_loop(0, tb // chunk, local_step, 0)

    def state_step(ci, carry):
        rows = chunk_rows(ci)
        st = [st_refs[p][...] for p in pairs]
        ar = [jnp.concatenate([stack(a_s[rows, s]), stack(r_s[rows, s])], axis=0) for s in lanes]
        ars = [_dot_nt(ar[p], st[p], hp=RWKV_HP_STATE) for p in pairs]
        u_st = [_dot(tinv_s[ci, p], ars[p][:c2] + lva_s[ci, p], hp=RWKV_HP_STATE) for p in pairs]
        y_st = [ars[p][c2:] + lvr_s[ci, p] + _dot(lrb_s[ci, p], u_st[p], hp=RWKV_HP_STATE) for p in pairs]
        new = []
        for p in pairs:
            s = lanes[p]
            uv = jnp.concatenate([u_st[p], stack(v_s[rows, s])], axis=0)
            bk = jnp.concatenate([stack(bp_s[rows, s]), stack(kp_s[rows, s])], axis=0)
            pc = pc_s[pl.ds(ci * chunk, 1), s]
            new.append(st[p] * pc + _dot_tn(uv, bk, hp=RWKV_HP_STATE))
        for p in pairs:
            y_s[rows, lanes[p]] = y_st[p][:chunk] + y_st[p][chunk:]
            st_refs[p][...] = new[p]
        return carry

    lax.fori_loop(0, tb // chunk, state_step, 0)

    for p in range(npair):
        sl = slice(p * LANES, (p + 1) * LANES)
        y = y_s[:, sl]
        mu = _pair_sum(y, lo_full) * (1.0 / RWKV_N)
        d = y - mu
        var = _pair_sum(d * d, lo_full) * (1.0 / RWKV_N)
        yn = (d * lax.rsqrt(var + 64e-5)) * lnw_ref[:, sl] + lnb_ref[:, sl]
        o_ref[0, :, sl] = (yn + bonus[p]) * gate[:, sl]

    @pl.when(blk == pl.num_programs(1) - 1)
    def _():
        for p in range(npair):
            s_ref[0, p] = st_refs[p][...]


def _rwkv(pb, prev, s0_bd, mu, wlr, w0, a0, k_k, k_a, r_k, ln_w, ln_b, tb, chunk):
    bsz, seq, _ = pb.shape
    npair = RWKV_HEADS // 2
    vec = lambda n: pl.BlockSpec((1, n), lambda b, i: (0, 0))
    scr = lambda: pltpu.VMEM((tb, RWKV_W), F32)
    nch, c2 = tb // chunk, 2 * chunk
    return pl.pallas_call(
        functools.partial(_rwkv_kernel, tb, chunk),
        grid=(bsz, seq // tb),
        in_specs=[
            pl.BlockSpec((1, tb, B_COLS), lambda b, i: (b, i, 0)),
            pl.BlockSpec((1, 1, B_COLS), lambda b, i: (b, 0, 0)),
            pl.BlockSpec((1, npair, LANES, LANES), lambda b, i: (b, 0, 0, 0)),
            vec(B_COLS),
            pl.BlockSpec(wlr.shape, lambda b, i: (0, 0)),
            vec(RWKV_W), vec(RWKV_W), vec(RWKV_W), vec(RWKV_W), vec(RWKV_W), vec(RWKV_W), vec(RWKV_W),
        ],
        out_specs=[
            pl.BlockSpec((1, tb, RWKV_W), lambda b, i: (b, i, 0)),
            pl.BlockSpec((1, npair, LANES, LANES), lambda b, i: (b, 0, 0, 0)),
        ],
        out_shape=[
            jax.ShapeDtypeStruct((bsz, seq, RWKV_W), F32),
            jax.ShapeDtypeStruct((bsz, npair, LANES, LANES), F32),
        ],
        scratch_shapes=[
            pltpu.VMEM((LANES, LANES), F32), pltpu.VMEM((LANES, LANES), F32),
            pltpu.VMEM((LANES, LANES), F32), pltpu.VMEM((LANES, LANES), F32),
            pltpu.VMEM((1, B_COLS), F32),
            scr(), scr(), scr(), scr(), scr(), scr(), scr(), scr(), scr(),
            pltpu.VMEM((nch, npair, c2, c2), F32), pltpu.VMEM((nch, npair, c2, c2), F32),
            pltpu.VMEM((nch, npair, c2, LANES), F32), pltpu.VMEM((nch, npair, c2, LANES), F32),
        ],
        compiler_params=_cparams(("parallel", "arbitrary")),
        name="rwkv7",
    )(pb, prev, s0_bd, mu.reshape(1, -1), wlr, w0.reshape(1, -1), a0.reshape(1, -1), k_k.reshape(1, -1),
      k_a.reshape(1, -1), r_k.reshape(1, -1), ln_w.reshape(1, -1), ln_b.reshape(1, -1))


def _sb_block(z, mask, carry, tri_u):
    log_beta = jnp.minimum(z, 0.0) - jnp.log1p(jnp.exp(-jnp.abs(z)))
    log_rem = log_beta - z
    if mask is not None:
        log_rem = jnp.where(mask, log_rem, 0.0)
    hi = log_rem.astype(BF16)
    lo = (log_rem - hi.astype(F32)).astype(BF16)
    after = jnp.dot(jnp.concatenate([hi, lo], axis=1), tri_u, preferred_element_type=F32)
    w = jnp.exp(log_beta + (after + carry))
    if mask is not None:
        w = jnp.where(mask, w, 0.0)
    return w, carry + jnp.sum(log_rem, axis=-1, keepdims=True)


def _strict_upper(n):
    r = lax.broadcasted_iota(jnp.int32, (2 * n, n), 0) % n
    c = lax.broadcasted_iota(jnp.int32, (2 * n, n), 1)
    return (r > c).astype(BF16)


def _sb_prompt_kernel(tq, tk, q_ref, k_ref, v_ref, o_ref):
    qi = pl.program_id(2)
    q = q_ref[0] * (SB_DH ** -0.5)
    lo = _lo_mask((tq, LANES))
    qa = jnp.where(lo, q, 0.0).astype(BF16)
    qb = jnp.where(lo, 0.0, q).astype(BF16)
    tri_u = _strict_upper(tk)
    qpos = qi * tq + lax.broadcasted_iota(jnp.int32, (tq, tk), 0)
    kofs = lax.broadcasted_iota(jnp.int32, (tq, tk), 1)
    n_diag = tq // tk
    n_full = qi * n_diag

    def visit(kb_idx, state, masked):
        acc_a, acc_b, car_a, car_b = state
        rows = pl.ds(pl.multiple_of(kb_idx * tk, tk), tk)
        kblk = k_ref[0, rows, :].astype(BF16)
        vblk = v_ref[0, rows, :].astype(BF16)
        mask = ((kb_idx * tk + kofs) < qpos) if masked else None
        w_a, car_a = _sb_block(_dot_nt(qa, kblk), mask, car_a, tri_u)
        w_b, car_b = _sb_block(_dot_nt(qb, kblk), mask, car_b, tri_u)
        acc_a = acc_a + jnp.dot(w_a.astype(BF16), vblk, preferred_element_type=F32)
        acc_b = acc_b + jnp.dot(w_b.astype(BF16), vblk, preferred_element_type=F32)
        return acc_a, acc_b, car_a, car_b

    zero = jnp.zeros((tq, LANES), F32)
    zc = jnp.zeros((tq, 1), F32)
    state = (zero, zero, zc, zc)
    for d in reversed(range(n_diag)):
        state = visit(n_full + d, state, True)
    state = lax.fori_loop(0, n_full, lambda j, s: visit(n_full - 1 - j, s, False), state)
    o_ref[0] = jnp.where(lo, state[0], state[1])


def _sb_prompt(q, k, v, tq, tk):
    bsz, seq, _ = q.shape
    npair = SB_HEADS // 2
    return pl.pallas_call(
        functools.partial(_sb_prompt_kernel, tq, tk),
        grid=(bsz, npair, seq // tq),
        in_specs=[
            pl.BlockSpec((1, tq, LANES), lambda b, h, i: (b, i, h)),
            pl.BlockSpec((1, seq, LANES), lambda b, h, i: (b, 0, h)),
            pl.BlockSpec((1, seq, LANES), lambda b, h, i: (b, 0, h)),
        ],
        out_specs=pl.BlockSpec((1, tq, LANES), lambda b, h, i: (b, i, h)),
        out_shape=jax.ShapeDtypeStruct((bsz, seq, SB_W), F32),
        compiler_params=_cparams(("parallel", "parallel", "arbitrary")),
        name="sb_prompt",
    )(q, k, v)


def _sb_sample_kernel(npg, lq, *refs):
    pt_ref = refs[0]
    q_ref, kn_ref, vn_ref = refs[1:4]
    kp_refs = refs[4:4 + npg]
    vp_refs = refs[4 + npg:4 + 2 * npg]
    o_ref = refs[4 + 2 * npg]
    qbd_ref, acc_ref, car_ref = refs[5 + 2 * npg:]
    del pt_ref
    j = pl.program_id(1)
    rows = SB_HEADS * lq
    tri_u = _strict_upper(PAGE)

    def visit(kblk, vblk, mask, feature_major):
        z = _dot(qbd_ref[...], kblk) if feature_major else _dot_nt(qbd_ref[...], kblk)
        w, car = _sb_block(z, mask, car_ref[...], tri_u)
        car_ref[...] = car
        acc_ref[...] += _dot_nt(w, vblk) if feature_major else _dot(w, vblk)

    @pl.when(j == 0)
    def _():
        q = q_ref[0] * (SB_DH ** -0.5)
        qt = jnp.concatenate([q] * SB_HEADS, axis=0)
        rh = lax.broadcasted_iota(jnp.int32, (rows, SB_W), 0) // lq
        ch = lax.broadcasted_iota(jnp.int32, (rows, SB_W), 1) // SB_DH
        qbd_ref[...] = jnp.where(rh == ch, qt, 0.0).astype(BF16)
        acc_ref[...] = jnp.zeros_like(acc_ref)
        car_ref[...] = jnp.zeros_like(car_ref)
        pad = jnp.zeros((PAGE - lq, SB_W), F32)
        kblk = jnp.concatenate([kn_ref[0], pad], axis=0)
        vblk = jnp.concatenate([vn_ref[0], pad], axis=0)
        qidx = lax.broadcasted_iota(jnp.int32, (rows, PAGE), 0) % lq
        kidx = lax.broadcasted_iota(jnp.int32, (rows, PAGE), 1)
        visit(kblk, vblk, kidx < qidx, False)

    full = jnp.full((rows, PAGE), True)
    for kp_ref, vp_ref in zip(kp_refs, vp_refs):
        visit(kp_ref[0, 0], vp_ref[0, 0], full, True)

    @pl.when(j == pl.num_programs(1) - 1)
    def _():
        acc = acc_ref[...]
        ch = lax.broadcasted_iota(jnp.int32, (lq, SB_W), 1) // SB_DH
        out = jnp.zeros((lq, SB_W), F32)
        for h in range(SB_HEADS):
            out = out + jnp.where(ch == h, acc[h * lq:(h + 1) * lq, :], 0.0)
        o_ref[0] = out


def _sb_sample(q, k_new, v_new, cache_k, cache_v, page_table, npg):
    bsz, lq, _ = q.shape
    n_pages = page_table.shape[1]
    rows = SB_HEADS * lq
    tok = pl.BlockSpec((1, lq, SB_W), lambda b, j, pt: (b, 0, 0))

    def page_spec(i):
        return pl.BlockSpec((1, 1, SB_W, PAGE), lambda b, j, pt: (0, pt[b, n_pages - 1 - (j * npg + i)], 0, 0))

    grid_spec = pltpu.PrefetchScalarGridSpec(
        num_scalar_prefetch=1,
        grid=(bsz, n_pages // npg),
        in_specs=[tok, tok, tok] + [page_spec(i) for i in range(npg)] * 2,
        out_specs=pl.BlockSpec((1, lq, SB_W), lambda b, j, pt: (b, 0, 0)),
        scratch_shapes=[
            pltpu.VMEM((rows, SB_W), BF16),
            pltpu.VMEM((rows, SB_W), F32),
            pltpu.VMEM((rows, 1), F32),
        ],
    )
    return pl.pallas_call(
        functools.partial(_sb_sample_kernel, npg, lq),
        grid_spec=grid_spec,
        out_shape=jax.ShapeDtypeStruct((bsz, lq, SB_W), F32),
        compiler_params=_cparams(("parallel", "arbitrary")),
        name="sb_sample",
    )(page_table, q, k_new, v_new, *([cache_k] * npg), *([cache_v] * npg))


def _first_index(cond, idx, big):
    return jnp.min(jnp.where(cond, idx, big), axis=(0, 1), keepdims=True)


def _router_kernel(x_ref, g_ref, sh_ref, sc_ref, rw_ref, rb_ref, h_ref, comb_ref, cnt_ref):
    h = _modulated_norm(x_ref[0], g_ref[...], sh_ref[0], sc_ref[0])
    h_ref[0] = h.astype(BF16)
    tl = h.shape[0]
    logits = _dot_nt(rw_ref[...], h, hp=True)
    scores = jax.nn.sigmoid(logits).reshape(N_GROUPS, GROUP_SIZE, tl)
    biased = scores + rb_ref[...].reshape(N_GROUPS, GROUP_SIZE, 1)
    neg = -jnp.inf

    jidx = lax.broadcasted_iota(jnp.int32, biased.shape, 1)
    m1 = jnp.max(biased, axis=1, keepdims=True)
    first = jnp.min(jnp.where(biased == m1, jidx, GROUP_SIZE), axis=1, keepdims=True)
    m2 = jnp.max(jnp.where(jidx == first, neg, biased), axis=1, keepdims=True)
    gscore = m1 + m2

    gidx = lax.broadcasted_iota(jnp.int32, gscore.shape, 0)
    gsel = jnp.zeros(gscore.shape, jnp.bool_)
    for _ in range(TOPK_GROUPS):
        m = jnp.max(gscore, axis=0, keepdims=True)
        pick = gidx == jnp.min(jnp.where(gscore == m, gidx, N_GROUPS), axis=0, keepdims=True)
        gsel = jnp.logical_or(gsel, pick)
        gscore = jnp.where(pick, neg, gscore)

    cand = jnp.where(gsel, biased, neg)
    eidx = lax.broadcasted_iota(jnp.int32, cand.shape, 0) * GROUP_SIZE + jidx
    esel = jnp.zeros(cand.shape, jnp.bool_)
    for _ in range(TOP_K):
        m = jnp.max(cand, axis=(0, 1), keepdims=True)
        pick = eidx == _first_index(cand == m, eidx, N_EXPERTS)
        esel = jnp.logical_or(esel, pick)
        cand = jnp.where(pick, neg, cand)

    sel = jnp.where(esel, scores, 0.0)
    wts = (sel / jnp.sum(sel, axis=(0, 1), keepdims=True) * ROUTED_SCALE).reshape(N_EXPERTS, tl)
    comb_ref[0] = wts
    cnt = jnp.sum((wts > 0.0).astype(jnp.int32), axis=-1, keepdims=True)
    cnt_ref[0] = jnp.broadcast_to(cnt, (N_EXPERTS, LANES))


def _router(x3, g, sh3, sc3, router_w, router_bias):
    gsz, lg, _ = x3.shape
    tl = MOE_TB
    tpg = lg // tl
    lm = sh3.shape[1]
    tm = 1 if lm == 1 else tl
    mod_map = (lambda b, i: (b, 0, 0)) if lm == 1 else (lambda b, i: (b, i, 0))
    return pl.pallas_call(
        _router_kernel,
        grid=(gsz, tpg),
        in_specs=[
            pl.BlockSpec((1, tl, D_MODEL), lambda b, i: (b, i, 0)),
            pl.BlockSpec((1, D_MODEL), lambda b, i: (0, 0)),
            pl.BlockSpec((1, tm, D_MODEL), mod_map),
            pl.BlockSpec((1, tm, D_MODEL), mod_map),
            pl.BlockSpec((N_EXPERTS, D_MODEL), lambda b, i: (0, 0)),
            pl.BlockSpec((N_EXPERTS, 1), lambda b, i: (0, 0)),
        ],
        out_specs=[
            pl.BlockSpec((1, tl, D_MODEL), lambda b, i: (b, i, 0)),
            pl.BlockSpec((1, N_EXPERTS, tl), lambda b, i: (b * tpg + i, 0, 0)),
            pl.BlockSpec((1, N_EXPERTS, LANES), lambda b, i: (b * tpg + i, 0, 0)),
        ],
        out_shape=[
            jax.ShapeDtypeStruct((gsz, lg, D_MODEL), BF16),
            jax.ShapeDtypeStruct((gsz * tpg, N_EXPERTS, tl), F32),
            jax.ShapeDtypeStruct((gsz * tpg, N_EXPERTS, LANES), jnp.int32),
        ],
        compiler_params=_cparams(("parallel", "parallel")),
        name="moe_router",
    )(x3, g.reshape(1, D_MODEL), sh3, sc3, router_w.T, router_bias.reshape(N_EXPERTS, 1))


def _swiglu_hidden(h, wg, wu):
    return _silu(jnp.dot(h, wg, preferred_element_type=F32)) * jnp.dot(h, wu, preferred_element_type=F32)


def _moe_plan(cnt, nt_max):
    e_ids = N_EXPERTS
    pc = (cnt + MOE_UNIT - 1) // MOE_UNIT * MOE_UNIT
    lend = jnp.cumsum(pc, axis=1)
    lstart = lend - pc
    nun = (lend[:, -1] // MOE_UNIT).astype(jnp.int32)
    ctot = jnp.sum(pc, axis=0)
    rs = (ctot + MOE_RT - 1) // MOE_RT * MOE_RT
    gend = jnp.cumsum(rs)
    gstart = gend - rs
    toff = jnp.cumsum(pc, axis=0) - pc
    urow = jnp.arange(MOE_NU, dtype=jnp.int32) * MOE_UNIT
    eu = jnp.sum((urow[None, :, None] >= lend[:, None, :]).astype(jnp.int32), axis=-1)
    valid = eu < e_ids
    euc = jnp.minimum(eu, e_ids - 1)
    take = lambda a: jnp.take_along_axis(a, euc, axis=1)
    off = urow[None, :] - take(lstart)
    dst = (gstart[euc] + take(toff) + off) // MOE_UNIT
    uinfo = jnp.where(valid, eu * MOE_INFO_SHIFT + off, e_ids * MOE_INFO_SHIFT)
    dst = jnp.where(valid, dst, 0)
    zn = ((rs - ctot) // MOE_UNIT).astype(jnp.int32)
    zdst = (gstart + ctot)[:, None] // MOE_UNIT + jnp.arange(MOE_RT // MOE_UNIT, dtype=jnp.int32)[None, :]
    n_used = (gend[-1] // MOE_RT).astype(jnp.int32).reshape(1)
    tmap = jnp.sum((jnp.arange(nt_max, dtype=jnp.int32)[:, None] * MOE_RT >= gend[None, :]).astype(jnp.int32), axis=-1)
    tmap = jnp.minimum(tmap, e_ids - 1)
    i32 = lambda a: a.astype(jnp.int32).reshape(-1)
    return dict(uinfo=i32(uinfo), dst=i32(dst), nun=nun, zn=zn, zdst=i32(zdst), tmap=i32(tmap), n_used=n_used)


def _routing_rank(sel, rank_s, wsel_s=None):
    tb = sel.shape[1]
    mask = sel > 0.0
    before = (lax.broadcasted_iota(jnp.int32, (tb, tb), 0) < lax.broadcasted_iota(jnp.int32, (tb, tb), 1))
    rank = jnp.dot(mask.astype(BF16), before.astype(BF16), preferred_element_type=F32)
    rank_s[0:N_EXPERTS, :] = jnp.where(mask, rank, -1.0)
    rank_s[N_EXPERTS:, :] = jnp.full((8, tb), -1.0, F32)
    if wsel_s is not None:
        wsel_s[0:N_EXPERTS, :] = sel
        wsel_s[N_EXPERTS:, :] = jnp.zeros((8, tb), F32)


def _unit_rows(uinfo_ref, base, u, rank_s):
    info = uinfo_ref[base + u]
    e = lax.shift_right_logical(info, MOE_INFO_BITS)
    off = jnp.bitwise_and(info, MOE_INFO_SHIFT - 1).astype(F32)
    j = lax.broadcasted_iota(jnp.int32, (MOE_UNIT, MOE_TB), 0).astype(F32)
    return rank_s[pl.ds(e, 1), :] == (j + off), e


def _unit_copy(src, dst, sem):
    return pltpu.make_async_copy(src, dst, sem)


def _moe_dispatch_kernel(uinfo_ref, dst_ref, nun_ref, zn_ref, zdst_ref, h_ref, sel_ref, xs_hbm,
                         p_s, xs_s, rank_s, zero_s, sem, zsem):
    i = pl.program_id(0)
    n = nun_ref[i]
    unit = lambda ref, u: ref.at[pl.ds(pl.multiple_of(u * MOE_UNIT, MOE_UNIT), MOE_UNIT)]

    @pl.when(i == 0)
    def _():
        zero_s[...] = jnp.zeros_like(zero_s)
        zper = MOE_RT // MOE_UNIT

        def per_expert(e, tot):
            def one(z, c):
                _unit_copy(zero_s, unit(xs_hbm, zdst_ref[e * zper + z]), zsem).start()
                return c
            lax.fori_loop(0, zn_ref[e], one, 0)
            return tot + zn_ref[e]

        total = lax.fori_loop(0, N_EXPERTS, per_expert, 0)

        def wait_one(z, c):
            _unit_copy(zero_s, unit(xs_hbm, 0), zsem).wait()
            return c
        lax.fori_loop(0, total, wait_one, 0)

    _routing_rank(sel_ref[0], rank_s)
    base = i * MOE_NU
    for u in range(MOE_NU):
        hit, _ = _unit_rows(uinfo_ref, base, u, rank_s)
        p_s[u * MOE_UNIT:(u + 1) * MOE_UNIT, :] = hit.astype(BF16)

    h = h_ref[0]
    for c in range(MOE_RMAX // MOE_CH):
        @pl.when(c * (MOE_CH // MOE_UNIT) < n)
        def _():
            rows = slice(c * MOE_CH, (c + 1) * MOE_CH)
            xs_s[rows, :] = jnp.dot(p_s[rows, :], h, preferred_element_type=F32).astype(BF16)

    def issue(u, c):
        _unit_copy(unit(xs_s, u), unit(xs_hbm, dst_ref[base + u]), sem).start()
        return c
    lax.fori_loop(0, n, issue, 0)

    def wait_one(u, c):
        _unit_copy(unit(xs_s, 0), unit(xs_hbm, 0), sem).wait()
        return c
    lax.fori_loop(0, n, wait_one, 0)


def _moe_dispatch(h3, sel, plan, pmax):
    gsz, lg, _ = h3.shape
    tpg = lg // MOE_TB
    n_tiles = gsz * tpg
    grid_spec = pltpu.PrefetchScalarGridSpec(
        num_scalar_prefetch=5,
        grid=(n_tiles,),
        in_specs=[
            pl.BlockSpec((1, MOE_TB, D_MODEL), lambda i, *_: (i // tpg, i % tpg, 0)),
            pl.BlockSpec((1, N_EXPERTS, MOE_TB), lambda i, *_: (i, 0, 0)),
        ],
        out_specs=pl.BlockSpec(memory_space=pl.ANY),
        scratch_shapes=[
            pltpu.VMEM((MOE_RMAX, MOE_TB), BF16),
            pltpu.VMEM((MOE_RMAX, D_MODEL), BF16),
            pltpu.VMEM((N_EXPERTS + 8, MOE_TB), F32),
            pltpu.VMEM((MOE_UNIT, D_MODEL), BF16),
            pltpu.SemaphoreType.DMA(()),
            pltpu.SemaphoreType.DMA(()),
        ],
    )
    return pl.pallas_call(
        _moe_dispatch_kernel,
        grid_spec=grid_spec,
        out_shape=jax.ShapeDtypeStruct((pmax, D_MODEL), BF16),
        compiler_params=_cparams(("arbitrary",)),
        name="moe_dispatch",
    )(plan['uinfo'], plan['dst'], plan['nun'], plan['zn'], plan['zdst'], h3, sel)


def _moe_group_kernel(tmap_ref, nused_ref, xs_ref, wg_ref, wu_ref, wd_ref, ys_ref, wg_s, wu_s, wd_s):
    j = pl.program_id(0)

    @pl.when(j < nused_ref[0])
    def _():
        @pl.when(jnp.logical_or(j == 0, tmap_ref[j] != tmap_ref[jnp.maximum(j - 1, 0)]))
        def _():
            wg_s[...] = wg_ref[0].astype(BF16)
            wu_s[...] = wu_ref[0].astype(BF16)
            wd_s[...] = wd_ref[0].astype(BF16)

        act = _swiglu_hidden(xs_ref[...], wg_s[...], wu_s[...])
        ys_ref[...] = jnp.dot(act.astype(BF16), wd_s[...], preferred_element_type=F32).astype(BF16)


def _moe_group(xs, plan, wg, wu, wd):
    pmax = xs.shape[0]
    nt = pmax // MOE_RT
    row_map = lambda j, tmap, nused: (jnp.minimum(j, nused[0] - 1), 0)
    wmap = lambda j, tmap, nused: (tmap[j], 0, 0)
    grid_spec = pltpu.PrefetchScalarGridSpec(
        num_scalar_prefetch=2,
        grid=(nt,),
        in_specs=[
            pl.BlockSpec((MOE_RT, D_MODEL), row_map),
            pl.BlockSpec((1, D_MODEL, HIDDEN), wmap),
            pl.BlockSpec((1, D_MODEL, HIDDEN), wmap),
            pl.BlockSpec((1, HIDDEN, D_MODEL), wmap),
        ],
        out_specs=pl.BlockSpec((MOE_RT, D_MODEL), row_map),
        scratch_shapes=[
            pltpu.VMEM((D_MODEL, HIDDEN), BF16),
            pltpu.VMEM((D_MODEL, HIDDEN), BF16),
            pltpu.VMEM((HIDDEN, D_MODEL), BF16),
        ],
    )
    return pl.pallas_call(
        _moe_group_kernel,
        grid_spec=grid_spec,
        out_shape=jax.ShapeDtypeStruct((pmax, D_MODEL), BF16),
        compiler_params=_cparams(("arbitrary",)),
        name="moe_group",
    )(plan['tmap'], plan['n_used'], xs, wg, wu, wd)


def _moe_combine_kernel(final, uinfo_ref, dst_ref, nun_ref, h_ref, sel_ref, x_ref, gate_ref, fn_ref,
                        sg_ref, su_ref, sd_ref, ys_hbm, o_ref, pw_s, ys_s, rank_s, wsel_s, acc_s, sem):
    i = pl.program_id(0)
    n = nun_ref[i]
    base = i * MOE_NU
    unit = lambda ref, u: ref.at[pl.ds(pl.multiple_of(u * MOE_UNIT, MOE_UNIT), MOE_UNIT)]

    @pl.when(i == 0)
    def _():
        ys_s[...] = jnp.zeros_like(ys_s)

    def issue(u, c):
        _unit_copy(unit(ys_hbm, dst_ref[base + u]), unit(ys_s, u), sem).start()
        return c
    lax.fori_loop(0, n, issue, 0)

    _routing_rank(sel_ref[0], rank_s, wsel_s)
    for u in range(MOE_NU):
        hit, e = _unit_rows(uinfo_ref, base, u, rank_s)
        w = jnp.where(hit, wsel_s[pl.ds(e, 1), :], 0.0)
        hi = w.astype(BF16)
        rows = slice(u * MOE_UNIT, (u + 1) * MOE_UNIT)
        pw_s[rows, 0:MOE_TB] = hi
        pw_s[rows, MOE_TB:] = (w - hi.astype(F32)).astype(BF16)

    h = h_ref[0]
    act = _swiglu_hidden(h, sg_ref[...], su_ref[...])
    acc_s[...] = jnp.dot(act.astype(BF16), sd_ref[...], preferred_element_type=F32)

    def wait_one(u, c):
        _unit_copy(unit(ys_hbm, 0), unit(ys_s, 0), sem).wait()
        return c
    lax.fori_loop(0, n, wait_one, 0)

    for c in range(MOE_RMAX // MOE_CH):
        @pl.when(c * (MOE_CH // MOE_UNIT) < n)
        def _():
            rows = slice(c * MOE_CH, (c + 1) * MOE_CH)
            r = _dot_tn(pw_s[rows, :], ys_s[rows, :])
            acc_s[...] += r[:MOE_TB] + r[MOE_TB:]

    y = x_ref[0] + gate_ref[0] * acc_s[...]
    if final:
        y = (y * lax.rsqrt(jnp.mean(y * y, axis=-1, keepdims=True) + EPS)) * fn_ref[...]
    o_ref[0] = y


def _moe_combine(x3, gate3, h3, sel, ys, plan, sg, su, sd, final_norm, final):
    gsz, lg, _ = x3.shape
    tpg = lg // MOE_TB
    lm = gate3.shape[1]
    tm = 1 if lm == 1 else MOE_TB
    tok_map = lambda i, *_: (i // tpg, i % tpg, 0)
    mod_map = (lambda i, *_: (i // tpg, 0, 0)) if lm == 1 else tok_map
    full = lambda a: pl.BlockSpec(a.shape, lambda i, *_: (0, 0))
    grid_spec = pltpu.PrefetchScalarGridSpec(
        num_scalar_prefetch=3,
        grid=(gsz * tpg,),
        in_specs=[
            pl.BlockSpec((1, MOE_TB, D_MODEL), tok_map),
            pl.BlockSpec((1, N_EXPERTS, MOE_TB), lambda i, *_: (i, 0, 0)),
            pl.BlockSpec((1, MOE_TB, D_MODEL), tok_map),
            pl.BlockSpec((1, tm, D_MODEL), mod_map),
            pl.BlockSpec((1, D_MODEL), lambda i, *_: (0, 0)),
            full(sg), full(su), full(sd),
            pl.BlockSpec(memory_space=pl.ANY),
        ],
        out_specs=pl.BlockSpec((1, MOE_TB, D_MODEL), tok_map),
        scratch_shapes=[
            pltpu.VMEM((MOE_RMAX, 2 * MOE_TB), BF16),
            pltpu.VMEM((MOE_RMAX, D_MODEL), BF16),
            pltpu.VMEM((N_EXPERTS + 8, MOE_TB), F32),
            pltpu.VMEM((N_EXPERTS + 8, MOE_TB), F32),
            pltpu.VMEM((MOE_TB, D_MODEL), F32),
            pltpu.SemaphoreType.DMA(()),
        ],
    )
    return pl.pallas_call(
        functools.partial(_moe_combine_kernel, final),
        grid_spec=grid_spec,
        out_shape=jax.ShapeDtypeStruct(x3.shape, F32),
        compiler_params=_cparams(("arbitrary",)),
        name="moe_combine",
    )(plan['uinfo'], plan['dst'], plan['nun'], h3, sel, x3, gate3, final_norm.reshape(1, D_MODEL), sg, su, sd, ys)


def _block_diag_pairs(s):
    bsz, nh, n, _ = s.shape
    s = s.reshape(bsz, nh // 2, 2, n, n)
    z = jnp.zeros_like(s[:, :, 0])
    top = jnp.concatenate([s[:, :, 0], z], axis=-1)
    bot = jnp.concatenate([z, s[:, :, 1]], axis=-1)
    return jnp.concatenate([top, bot], axis=-2)


def _unpair(s_bd):
    bsz, npair, n2, _ = s_bd.shape
    n = n2 // 2
    return jnp.stack([s_bd[:, :, :n, :n], s_bd[:, :, n:, n:]], axis=2).reshape(bsz, npair * 2, n, n)


def _rotary_tables(pos):
    half = RET_DK // 2
    inv = ROPE_BASE ** (-jnp.arange(half, dtype=F32) / half)
    ang = pos.astype(F32)[:, None] * inv[None, :]
    cos, sin = jnp.cos(ang), jnp.sin(ang)
    cos_t = jnp.concatenate([cos, cos, cos, cos], axis=-1)
    sin_t = jnp.concatenate([-sin, sin, -sin, sin], axis=-1)
    return cos_t, sin_t


def _trunk(x, c, pos, ret_s, rwkv_s, shift_s, cache, p, flat):
    bsz, seq, _ = x.shape
    mod = _ada(c, p['ada_w'], p['ada_b'])
    mod = mod.reshape(DEPTH, bsz, 6, D_MODEL)
    if flat:
        tok = lambda t: t.reshape(1, bsz * seq, t.shape[-1])
        untok = lambda t: t.reshape(bsz, seq, t.shape[-1])
        modv = lambda l, j: jnp.broadcast_to(mod[l, :, j][:, None, :], (bsz, seq, D_MODEL)).reshape(1, bsz * seq, D_MODEL)
        tl_proj = bsz * seq
    else:
        tok = untok = lambda t: t
        modv = lambda l, j: mod[l, :, j][:, None, :]
        tl_proj = min(256, seq)

    x3 = tok(x)
    w_in = p['w_in_ab'][0].astype(BF16)
    a_cols = 4 * RET_W
    zeros_a = jnp.zeros((1, a_cols), F32)
    zeros_b = jnp.zeros((1, B_COLS), F32)
    pa, pb = _norm_proj(x3, p['norm_mix'][0], modv(0, 0), modv(0, 1), [w_in[:, :a_cols], w_in[:, a_cols:]],
                        [zeros_a, zeros_b], tl_proj)
    pa, pb = untok(pa), untok(pb)
    cos_t, sin_t = _rotary_tables(pos)
    lgs = jnp.log1p(-jnp.exp2(-5.0 - jnp.arange(RET_HEADS, dtype=F32)))
    ret_chunk = math.gcd(seq, 128)
    o_a, ret_new = _retention(pa, cos_t, sin_t, lgs, p['ret_gn_w'][0], p['ret_gn_b'][0],
                              _block_diag_pairs(ret_s), ret_chunk)
    wlr = jnp.zeros((LOWRANK, 3 * RWKV_W), F32)
    wlr = wlr.at[0:64, 0:RWKV_W].set(p['rwkv_w_up'][0])
    wlr = wlr.at[64:128, RWKV_W:2 * RWKV_W].set(p['rwkv_a_up'][0])
    wlr = wlr.at[128:256, 2 * RWKV_W:].set(p['rwkv_g_up'][0])
    tb = min(seq, 256)
    o_b, rwkv_new = _rwkv(pb, shift_s[:, None, :], _block_diag_pairs(rwkv_s), p['rwkv_mu'][0], wlr.astype(BF16),
                          p['rwkv_w0'][0], p['rwkv_a0'][0], p['rwkv_k_k'][0], p['rwkv_k_a'][0],
                          p['rwkv_r_k'][0].reshape(-1), p['rwkv_ln_w'][0], p['rwkv_ln_b'][0],
                          tb, min(RWKV_CHUNK, seq))
    w_out = p['w_out_ab'][0].astype(BF16)
    x3 = _out_proj(x3, modv(0, 2), [tok(o_a), tok(o_b)], [w_out[:RET_W], w_out[RET_W:]], tl_proj)
    x3 = _moe_layer(x3, 0, modv, p, final=False)

    wq, wk, wv = jnp.split(p['w_qkv_c'][0].astype(BF16), 3, axis=1)
    q, k, v = _norm_proj(x3, p['norm_mix'][1], modv(1, 0), modv(1, 1), [wq, wk, wv],
                         [p['b_q_c'][0][None, :], p['b_k_c'][0][None, :], jnp.zeros((1, SB_W), F32)], tl_proj)
    q, k, v = untok(q), untok(k), untok(v)
    if cache is None:
        o = _sb_prompt(q, k, v, min(512, seq), min(256, seq))
    else:
        cache_k, cache_v, page_table = cache
        o = _sb_sample(q, k, v, cache_k, cache_v, page_table, 4)
    x3 = _out_proj(x3, modv(1, 2), [tok(o)], [p['w_out_c'][0].astype(BF16)], tl_proj)
    y3 = _moe_layer(x3, 1, modv, p, final=True)

    kv_shape = (1, bsz, seq, SB_HEADS, SB_DH)
    return (untok(y3), _unpair(ret_new)[None], _unpair(rwkv_new)[None], pb[:, -1][None],
            k.reshape(kv_shape), v.reshape(kv_shape))


def _moe_layer(x3, l, modv, p, final):
    gsz, lg, _ = x3.shape
    n_tiles = gsz * lg // MOE_TB
    pmax = gsz * lg * TOP_K + n_tiles * N_EXPERTS * (MOE_UNIT - 1) + N_EXPERTS * (MOE_RT - MOE_UNIT)
    pmax = -(-pmax // MOE_RT) * MOE_RT
    h3, sel, cnt = _router(x3, p['norm_ffn'][l], modv(l, 3), modv(l, 4), p['router_w'][l], p['router_bias'][l])
    plan = _moe_plan(cnt[:, :, 0], pmax // MOE_RT)
    xs = _moe_dispatch(h3, sel, plan, pmax)
    ys = _moe_group(xs, plan, p['exp_w_gate'][l], p['exp_w_up'][l], p['exp_w_down'][l])
    return _moe_combine(x3, modv(l, 5), h3, sel, ys, plan, p['sh_w_gate'][l].astype(BF16),
                        p['sh_w_up'][l].astype(BF16), p['sh_w_down'][l].astype(BF16), p['final_norm'], final)


def kernel(x_prompt, x_sample, c_prompt, c_sample, state_ret, state_rwkv, state_shift, cache_k, cache_v, page_table, ada_w, ada_b, norm_mix, norm_ffn, final_norm, w_in_ab, w_out_ab, ret_gn_w, ret_gn_b, rwkv_mu, rwkv_w0, rwkv_w_up, rwkv_a0, rwkv_a_up, rwkv_g_up, rwkv_k_k, rwkv_k_a, rwkv_r_k, rwkv_ln_w, rwkv_ln_b, w_qkv_c, b_q_c, b_k_c, w_out_c, router_w, router_bias, exp_w_gate, exp_w_up, exp_w_down, sh_w_gate, sh_w_up, sh_w_down):
    p = dict(ada_w=ada_w, ada_b=ada_b, norm_mix=norm_mix, norm_ffn=norm_ffn, final_norm=final_norm,
             w_in_ab=w_in_ab, w_out_ab=w_out_ab, ret_gn_w=ret_gn_w, ret_gn_b=ret_gn_b, rwkv_mu=rwkv_mu,
             rwkv_w0=rwkv_w0, rwkv_w_up=rwkv_w_up, rwkv_a0=rwkv_a0, rwkv_a_up=rwkv_a_up, rwkv_g_up=rwkv_g_up,
             rwkv_k_k=rwkv_k_k, rwkv_k_a=rwkv_k_a, rwkv_r_k=rwkv_r_k, rwkv_ln_w=rwkv_ln_w, rwkv_ln_b=rwkv_ln_b,
             w_qkv_c=w_qkv_c, b_q_c=b_q_c, b_k_c=b_k_c, w_out_c=w_out_c, router_w=router_w,
             router_bias=router_bias, exp_w_gate=exp_w_gate, exp_w_up=exp_w_up, exp_w_down=exp_w_down,
             sh_w_gate=sh_w_gate, sh_w_up=sh_w_up, sh_w_down=sh_w_down)
    bp, lp, _ = x_prompt.shape
    bs, ls, _ = x_sample.shape
    zeros = lambda *s: jnp.zeros(s, F32)
    y_p, ret_p, rwkv_p, shift_p, k_p, v_p = _trunk(
        x_prompt, c_prompt, jnp.arange(lp), zeros(bp, RET_HEADS, RET_DK, RET_DK),
        zeros(bp, RWKV_HEADS, RWKV_N, RWKV_N), zeros(bp, B_COLS), None, p, flat=False)
    n_pages = page_table.shape[1]
    n_pool = cache_k.shape[1]
    pages = lambda t: jnp.transpose(t, (0, 1, 3, 4, 2)).reshape(1, n_pool, SB_W, PAGE)
    cache = (pages(cache_k), pages(cache_v), page_table)
    y_s, ret_s, rwkv_s, shift_s, k_s, v_s = _trunk(
        x_sample, c_sample, n_pages * PAGE + jnp.arange(ls), state_ret[0], state_rwkv[0], state_shift[0],
        cache, p, flat=True)
    return (y_p, y_s, ret_p, ret_s, rwkv_p, rwkv_s, shift_p, shift_s, k_p, v_p, k_s, v_s)
```

```python
import functools
import math

import jax
import jax.numpy as jnp
from jax import lax
from jax.experimental import pallas as pl
from jax.experimental.pallas import tpu as pltpu

F32 = jnp.float32
BF16 = jnp.bfloat16
HIGHEST = lax.Precision.HIGHEST

D_MODEL = 1024
DEPTH = 2
PAGE = 128
RET_HEADS = 8
RET_DK = 64
RWKV_HEADS = 8
RWKV_N = 64
RWKV_W = RWKV_HEADS * RWKV_N
RET_W = RET_HEADS * RET_DK
LOWRANK = 256
B_COLS = 3 * RWKV_W + LOWRANK
SB_HEADS = 16
SB_DH = 64
SB_W = SB_HEADS * SB_DH
N_EXPERTS = 64
N_GROUPS = 8
GROUP_SIZE = N_EXPERTS // N_GROUPS
TOPK_GROUPS = 4
TOP_K = 8
HIDDEN = 256
ROUTED_SCALE = 2.5
EPS = 1e-6
ROPE_BASE = 10000.0
LANES = 128
VMEM_LIMIT = 56 * 1024 * 1024

MOE_TB = 256
MOE_UNIT_BITS = 4
MOE_UNIT = 1 << MOE_UNIT_BITS
MOE_RT = 512
MOE_CH = 512
MOE_RMAX = -(-(MOE_TB * TOP_K + N_EXPERTS * (MOE_UNIT - 1)) // MOE_CH) * MOE_CH
MOE_NU = MOE_RMAX // MOE_UNIT
MOE_INFO_BITS = 9
MOE_INFO_SHIFT = 1 << MOE_INFO_BITS

RWKV_CHUNK = 16
RWKV_NSEQ = 4
RWKV_HP_LOCAL = False
RWKV_HP_STATE = False


def _cparams(sem):
    return pltpu.CompilerParams(dimension_semantics=sem, vmem_limit_bytes=VMEM_LIMIT)


def _dot(a, b, hp=False):
    if hp:
        return jnp.dot(a.astype(F32), b.astype(F32), preferred_element_type=F32, precision=HIGHEST)
    return jnp.dot(a.astype(BF16), b.astype(BF16), preferred_element_type=F32)


def _dot_nt(a, b, hp=False):
    dn = (((1,), (1,)), ((), ()))
    if hp:
        return lax.dot_general(a.astype(F32), b.astype(F32), dn, preferred_element_type=F32, precision=HIGHEST)
    return lax.dot_general(a.astype(BF16), b.astype(BF16), dn, preferred_element_type=F32)


def _dot_tn(a, b, hp=False):
    dn = (((0,), (0,)), ((), ()))
    if hp:
        return lax.dot_general(a.astype(F32), b.astype(F32), dn, preferred_element_type=F32, precision=HIGHEST)
    return lax.dot_general(a.astype(BF16), b.astype(BF16), dn, preferred_element_type=F32)


def _split_bf16(x):
    hi = x.astype(BF16)
    return hi, x - hi.astype(F32)


def _dot_x3(a, b):
    ah, al = _split_bf16(a)
    bh, bl = _split_bf16(b)
    d = lambda x, y: jnp.dot(x, y.astype(BF16), preferred_element_type=F32)
    return d(ah, bh) + (d(ah, bl) + d(al.astype(BF16), bh))


def _dot_exact_lhs(a, b):
    a = a.astype(BF16)
    b1, r1 = _split_bf16(b)
    b2, r2 = _split_bf16(r1)
    d = lambda y: jnp.dot(a, y.astype(BF16), preferred_element_type=F32)
    return d(b1) + (d(b2) + d(r2))


def _silu(x):
    return x * jax.nn.sigmoid(x)


def _lo_mask(shape):
    return (lax.broadcasted_iota(jnp.int32, shape, len(shape) - 1) % LANES) < (LANES // 2)


def _pair_sum(x, lo):
    s_lo = jnp.sum(jnp.where(lo, x, 0.0), axis=-1, keepdims=True)
    s_hi = jnp.sum(jnp.where(lo, 0.0, x), axis=-1, keepdims=True)
    return jnp.where(lo, s_lo, s_hi)


def _ada_kernel(c_ref, w_ref, b_ref, o_ref):
    o_ref[0] = _dot(_silu(c_ref[...]), w_ref[0]) + b_ref[0]


def _ada(c, ada_w, ada_b):
    bsz = c.shape[0]
    n = ada_w.shape[-1]
    tn = D_MODEL
    return pl.pallas_call(
        _ada_kernel,
        grid=(DEPTH, n // tn),
        in_specs=[
            pl.BlockSpec((bsz, D_MODEL), lambda l, j: (0, 0)),
            pl.BlockSpec((1, D_MODEL, tn), lambda l, j: (l, 0, j)),
            pl.BlockSpec((1, 1, tn), lambda l, j: (l, 0, j)),
        ],
        out_specs=pl.BlockSpec((1, bsz, tn), lambda l, j: (l, 0, j)),
        out_shape=jax.ShapeDtypeStruct((DEPTH, bsz, n), F32),
        compiler_params=_cparams(("parallel", "parallel")),
        name="ada_mod",
    )(c, ada_w, ada_b.reshape(DEPTH, 1, n))


def _modulated_norm(x, g, sh, sc):
    y = x * lax.rsqrt(jnp.mean(x * x, axis=-1, keepdims=True) + EPS)
    return (y * g) * (1.0 + sc) + sh


def _norm_proj_kernel(n_out, x_ref, g_ref, sh_ref, sc_ref, *refs):
    w_refs, b_refs, o_refs = refs[:n_out], refs[n_out:2 * n_out], refs[2 * n_out:]
    h = _modulated_norm(x_ref[0], g_ref[...], sh_ref[0], sc_ref[0]).astype(BF16)
    for w_ref, b_ref, o_ref in zip(w_refs, b_refs, o_refs):
        o_ref[0] = jnp.dot(h, w_ref[...], preferred_element_type=F32) + b_ref[...]


def _norm_proj(x3, g, sh3, sc3, ws, bs, tl):
    gsz, lg, _ = x3.shape
    lm = sh3.shape[1]
    tm = 1 if lm == 1 else tl
    mod_map = (lambda b, i: (b, 0, 0)) if lm == 1 else (lambda b, i: (b, i, 0))
    n_out = len(ws)
    in_specs = [
        pl.BlockSpec((1, tl, D_MODEL), lambda b, i: (b, i, 0)),
        pl.BlockSpec((1, D_MODEL), lambda b, i: (0, 0)),
        pl.BlockSpec((1, tm, D_MODEL), mod_map),
        pl.BlockSpec((1, tm, D_MODEL), mod_map),
    ]
    in_specs += [pl.BlockSpec(w.shape, lambda b, i: (0, 0)) for w in ws]
    in_specs += [pl.BlockSpec(b.shape, lambda b, i: (0, 0)) for b in bs]
    return pl.pallas_call(
        functools.partial(_norm_proj_kernel, n_out),
        grid=(gsz, lg // tl),
        in_specs=in_specs,
        out_specs=[pl.BlockSpec((1, tl, w.shape[1]), lambda b, i: (b, i, 0)) for w in ws],
        out_shape=[jax.ShapeDtypeStruct((gsz, lg, w.shape[1]), F32) for w in ws],
        compiler_params=_cparams(("parallel", "parallel")),
        name="norm_proj",
    )(x3, g.reshape(1, D_MODEL), sh3, sc3, *ws, *bs)


def _out_proj_kernel(n_in, x_ref, gate_ref, *refs):
    o_refs, w_refs, y_ref = refs[:n_in], refs[n_in:2 * n_in], refs[2 * n_in]
    acc = None
    for o_ref, w_ref in zip(o_refs, w_refs):
        t = jnp.dot(o_ref[0].astype(BF16), w_ref[...], preferred_element_type=F32)
        acc = t if acc is None else acc + t
    y_ref[0] = x_ref[0] + gate_ref[0] * acc


def _out_proj(x3, gate3, os_, ws, tl):
    gsz, lg, _ = x3.shape
    lm = gate3.shape[1]
    tm = 1 if lm == 1 else tl
    mod_map = (lambda b, i: (b, 0, 0)) if lm == 1 else (lambda b, i: (b, i, 0))
    n_in = len(os_)
    in_specs = [
        pl.BlockSpec((1, tl, D_MODEL), lambda b, i: (b, i, 0)),
        pl.BlockSpec((1, tm, D_MODEL), mod_map),
    ]
    in_specs += [pl.BlockSpec((1, tl, o.shape[-1]), lambda b, i: (b, i, 0)) for o in os_]
    in_specs += [pl.BlockSpec(w.shape, lambda b, i: (0, 0)) for w in ws]
    return pl.pallas_call(
        functools.partial(_out_proj_kernel, n_in),
        grid=(gsz, lg // tl),
        in_specs=in_specs,
        out_specs=pl.BlockSpec((1, tl, D_MODEL), lambda b, i: (b, i, 0)),
        out_shape=jax.ShapeDtypeStruct(x3.shape, F32),
        compiler_params=_cparams(("parallel", "parallel")),
        name="out_proj",
    )(x3, gate3, *os_, *ws)


def _rot_half(x, lo32):
    return jnp.where(lo32, pltpu.roll(x, LANES - 32, 1), pltpu.roll(x, 32, 1))


def _retention_kernel(chunk, lgs_ref, q_ref, k_ref, v_ref, g_ref, cos_ref, sin_ref, gw_ref, gb_ref, s0_ref,
                      o_ref, s_ref, st_ref):
    npair = RET_HEADS // 2
    c = pl.program_id(1)

    @pl.when(c == 0)
    def _():
        st_ref[...] = s0_ref[0]

    lane = lax.broadcasted_iota(jnp.int32, (chunk, LANES), 1)
    lo = lane < (LANES // 2)
    lo32 = (lane % RET_DK) < (RET_DK // 2)
    cos = cos_ref[...]
    sin = sin_ref[...]
    ti = lax.broadcasted_iota(jnp.int32, (chunk, LANES), 0).astype(F32)
    ii = lax.broadcasted_iota(jnp.int32, (chunk, chunk), 0)
    jj = lax.broadcasted_iota(jnp.int32, (chunk, chunk), 1)
    diff = jnp.maximum(ii - jj, 0).astype(F32)
    causal = ii >= jj
    ri = lax.broadcasted_iota(jnp.int32, (LANES, LANES), 0)
    ci = lax.broadcasted_iota(jnp.int32, (LANES, LANES), 1)
    same_head = (ri < RET_DK) == (ci < RET_DK)

    def pair(p, st):
        sl = slice(p * LANES, (p + 1) * LANES)
        lg_a = lgs_ref[2 * p]
        lg_b = lgs_ref[2 * p + 1]
        lg = jnp.where(lo, lg_a, lg_b)
        q = q_ref[0, :, sl]
        k = k_ref[0, :, sl]
        q = q * cos + _rot_half(q, lo32) * sin
        k = (k * cos + _rot_half(k, lo32) * sin) * (RET_DK ** -0.5)
        kb = k.astype(BF16)
        vb = v_ref[0, :, sl].astype(BF16)

        def head(lg_h, sel):
            dmask = jnp.where(causal, jnp.exp(lg_h * diff), 0.0)
            sc = _dot_nt(jnp.where(sel, q, 0.0), kb) * dmask
            return _dot(sc, vb)

        o = jnp.where(lo, head(lg_a, lo), head(lg_b, jnp.logical_not(lo)))
        o = o + _dot(q * jnp.exp(lg * (ti + 1.0)), st)
        c_dec = jnp.exp(jnp.where(ri < RET_DK, lg_a, lg_b) * float(chunk))
        st_new = st * c_dec + jnp.where(same_head, _dot_tn(k * jnp.exp(lg * (chunk - 1.0 - ti)), vb), 0.0)
        mu = _pair_sum(o, lo) * (1.0 / RET_DK)
        d = o - mu
        var = _pair_sum(d * d, lo) * (1.0 / RET_DK)
        y = (d * lax.rsqrt(var + 1e-5)) * gw_ref[:, sl] + gb_ref[:, sl]
        return y * _silu(g_ref[0, :, sl]), st_new

    states = [st_ref[p] for p in range(npair)]
    results = [pair(p, states[p]) for p in range(npair)]
    for p, (y, st_new) in enumerate(results):
        o_ref[0, :, p * LANES:(p + 1) * LANES] = y
        st_ref[p] = st_new

    @pl.when(c == pl.num_programs(1) - 1)
    def _():
        for p, (_, st_new) in enumerate(results):
            s_ref[0, p] = st_new


def _retention(pa, cos_t, sin_t, lgs, gn_w, gn_b, s0_bd, chunk):
    bsz, seq, _ = pa.shape
    npair = RET_HEADS // 2
    nc = seq // chunk
    col = lambda j: pl.BlockSpec((1, chunk, RET_W), lambda b, c: (b, c, j))
    return pl.pallas_call(
        functools.partial(_retention_kernel, chunk),
        grid=(bsz, nc),
        in_specs=[
            pl.BlockSpec(memory_space=pltpu.SMEM),
            col(0), col(1), col(2), col(3),
            pl.BlockSpec((chunk, LANES), lambda b, c: (c, 0)),
            pl.BlockSpec((chunk, LANES), lambda b, c: (c, 0)),
            pl.BlockSpec((1, RET_W), lambda b, c: (0, 0)),
            pl.BlockSpec((1, RET_W), lambda b, c: (0, 0)),
            pl.BlockSpec((1, npair, LANES, LANES), lambda b, c: (b, 0, 0, 0)),
        ],
        out_specs=[
            pl.BlockSpec((1, chunk, RET_W), lambda b, c: (b, c, 0)),
            pl.BlockSpec((1, npair, LANES, LANES), lambda b, c: (b, 0, 0, 0)),
        ],
        out_shape=[
            jax.ShapeDtypeStruct((bsz, seq, RET_W), F32),
            jax.ShapeDtypeStruct((bsz, npair, LANES, LANES), F32),
        ],
        scratch_shapes=[pltpu.VMEM((npair, LANES, LANES), F32)],
        compiler_params=_cparams(("parallel", "arbitrary")),
        name="retention",
    )(lgs, pa, pa, pa, pa, cos_t, sin_t, gn_w.reshape(1, RET_W), gn_b.reshape(1, RET_W), s0_bd)


def _rwkv_kernel(nseq, tb, chunk, pb_ref, prev_ref, s0_ref, mu_ref, wlr_ref, w0_ref, a0_ref, kk_ref, ka_ref, rk_ref,
                 lnw_ref, lnb_ref, o_ref, s_ref, *scratch):
    npair = RWKV_HEADS // 2
    nchain = nseq * npair
    st_refs = scratch[:nchain]
    carry_ref, a_s, b_s, k_s, r_s, v_s, bp_s, kp_s, pc_s, y_s, tinv_s, lrb_s, lva_s, lvr_s = scratch[nchain:]
    blk = pl.program_id(1)

    @pl.when(blk == 0)
    def _():
        for s in range(nseq):
            for p in range(npair):
                st_refs[s * npair + p][...] = s0_ref[s, p]
            carry_ref[s] = prev_ref[s]

    lo_full = _lo_mask((tb, LANES))
    ti = lax.broadcasted_iota(jnp.int32, (tb, tb), 0)
    tj = lax.broadcasted_iota(jnp.int32, (tb, tb), 1)
    same_chunk = (ti // chunk) == (tj // chunk)
    sel = jnp.concatenate([jnp.logical_and(same_chunk, ti >= tj), same_chunk], axis=0).astype(F32)

    def prepare(s):
        rs = slice(s * tb, (s + 1) * tb)
        pb = pb_ref[s]
        row = lax.broadcasted_iota(jnp.int32, pb.shape, 0)
        prev = jnp.where(row == 0, carry_ref[s], pltpu.roll(pb, 1, 0))
        carry_ref[s] = pb[tb - 1:tb, :]
        pbs = pb + (prev - pb) * mu_ref[...]
        r = pbs[:, :RWKV_W]
        kb = pbs[:, RWKV_W:2 * RWKV_W]
        v = pbs[:, 2 * RWKV_W:3 * RWKV_W]
        tail = pbs[:, 3 * RWKV_W:]
        tl_lane = lax.broadcasted_iota(jnp.int32, tail.shape, 1)
        act = jnp.where(tl_lane < 64, jnp.tanh(tail), jnp.where(tl_lane < 128, tail, jax.nn.sigmoid(tail)))
        lr = _dot(act, wlr_ref[...])
        wz = -(w0_ref[...] + lr[:, :RWKV_W])
        softplus = jnp.maximum(wz, 0.0) + jnp.log1p(jnp.exp(-jnp.abs(wz)))
        logw = -jnp.exp(-softplus - 0.5)
        a = jax.nn.sigmoid(a0_ref[...] + lr[:, RWKV_W:2 * RWKV_W])
        kkf = kb * kk_ref[...]
        kb2 = kb * (1.0 + (a - 1.0) * ka_ref[...])
        bonus_src = r * kb2 * rk_ref[...]
        cums = _dot_exact_lhs(sel, logw)
        cum, tot = cums[:tb], cums[tb:]
        pin = jnp.exp(cum)
        pinv = jnp.exp(-cum)
        prem = jnp.exp(tot - cum)
        bonus = []
        for p in range(npair):
            sl = slice(p * LANES, (p + 1) * LANES)
            kf = kkf[:, sl]
            kn = kf * lax.rsqrt(jnp.maximum(_pair_sum(kf * kf, lo_full), 1e-24))
            bv = kn * a[:, sl]
            a_s[rs, sl] = -kn * jnp.exp(cum[:, sl] - logw[:, sl])
            b_s[rs, sl] = bv * pinv[:, sl]
            bp_s[rs, sl] = bv * prem[:, sl]
            bonus.append(_pair_sum(bonus_src[:, sl], lo_full) * v[:, sl])
        k_s[rs, :] = kb2 * pinv
        kp_s[rs, :] = kb2 * prem
        r_s[rs, :] = r * pin
        v_s[rs, :] = v
        pc_s[rs, :] = jnp.exp(tot)
        return bonus, lr[:, 2 * RWKV_W:]

    prepared = [prepare(s) for s in range(nseq)]

    c2 = 2 * chunk
    lo = _lo_mask((chunk, LANES))
    r2 = lax.broadcasted_iota(jnp.int32, (c2, c2), 0)
    q2 = lax.broadcasted_iota(jnp.int32, (c2, c2), 1)
    same = (r2 // chunk) == (q2 // chunk)
    strict = jnp.logical_and(same, (r2 % chunk) > (q2 % chunk))
    incl = jnp.logical_and(same, (r2 % chunk) >= (q2 % chunk))
    eye = (r2 == q2).astype(F32)
    n_sq = int(math.log2(chunk)) - 1
    pairs = range(nchain)
    lanes = [slice((c % npair) * LANES, (c % npair + 1) * LANES) for c in pairs]
    row0 = [(c // npair) * tb for c in pairs]

    def stack(x):
        return jnp.concatenate([jnp.where(lo, x, 0.0), jnp.where(lo, 0.0, x)], axis=0)

    def chunk_rows(ci):
        return [pl.ds(pl.multiple_of(row0[c] + ci * chunk, chunk), chunk) for c in pairs]

    def local_step(ci, carry):
        rows = chunk_rows(ci)
        ar = [jnp.concatenate([stack(a_s[rows[p], lanes[p]]), stack(r_s[rows[p], lanes[p]])], axis=0)
              for p in pairs]
        b2 = [b_s[rows[p], lanes[p]] for p in pairs]
        k2 = [k_s[rows[p], lanes[p]] for p in pairs]
        gb_ = [_dot_nt(ar[p], jnp.concatenate([b2[p], b2[p]], axis=0), hp=RWKV_HP_LOCAL) for p in pairs]
        gk_ = [_dot_nt(ar[p], jnp.concatenate([k2[p], k2[p]], axis=0), hp=RWKV_HP_LOCAL) for p in pairs]
        l_ab = [jnp.where(strict, gb_[p][:c2], 0.0) for p in pairs]
        tinv = [eye + l_ab[p] for p in pairs]
        xp = [_dot_x3(l_ab[p], l_ab[p]) for p in pairs]
        for _ in range(n_sq - 1):
            both = [_dot_x3(jnp.concatenate([tinv[p], xp[p]], axis=0), xp[p]) for p in pairs]
            tinv = [tinv[p] + both[p][:c2] for p in pairs]
            xp = [both[p][c2:] for p in pairs]
        tinv = [tinv[p] + _dot_x3(tinv[p], xp[p]) for p in pairs]
        for p in pairs:
            l_akrk = jnp.concatenate([jnp.where(strict, gk_[p][:c2], 0.0), jnp.where(incl, gk_[p][c2:], 0.0)], axis=0)
            lv = _dot(l_akrk, stack(v_s[rows[p], lanes[p]]), hp=RWKV_HP_LOCAL)
            tinv_s[ci, p] = tinv[p]
            lrb_s[ci, p] = jnp.where(incl, gb_[p][c2:], 0.0)
            lva_s[ci, p] = lv[:c2]
            lvr_s[ci, p] = lv[c2:]
        return carry

    lax.fori_loop(0, tb // chunk, local_step, 0)

    def state_step(ci, carry):
        rows = chunk_rows(ci)
        st = [st_refs[p][...] for p in pairs]
        ar = [jnp.concatenate([stack(a_s[rows[p], lanes[p]]), stack(r_s[rows[p], lanes[p]])], axis=0) for p in pairs]
        ars = [_dot_nt(ar[p], st[p], hp=RWKV_HP_STATE) for p in pairs]
        u_st = [_dot(tinv_s[ci, p], ars[p][:c2] + lva_s[ci, p], hp=RWKV_HP_STATE) for p in pairs]
        y_st = [ars[p][c2:] + lvr_s[ci, p] + _dot(lrb_s[ci, p], u_st[p], hp=RWKV_HP_STATE) for p in pairs]
        new = []
        for p in pairs:
            r, s = rows[p], lanes[p]
            uv = jnp.concatenate([u_st[p], stack(v_s[r, s])], axis=0)
            bk = jnp.concatenate([stack(bp_s[r, s]), stack(kp_s[r, s])], axis=0)
            pc = pc_s[pl.ds(row0[p] + ci * chunk, 1), s]
            new.append(st[p] * pc + _dot_tn(uv, bk, hp=RWKV_HP_STATE))
        for p in pairs:
            y_s[rows[p], lanes[p]] = y_st[p][:chunk] + y_st[p][chunk:]
            st_refs[p][...] = new[p]
        return carry

    lax.fori_loop(0, tb // chunk, state_step, 0)

    for s in range(nseq):
        bonus, gate = prepared[s]
        for p in range(npair):
            sl = slice(p * LANES, (p + 1) * LANES)
            y = y_s[s * tb:(s + 1) * tb, sl]
            mu = _pair_sum(y, lo_full) * (1.0 / RWKV_N)
            d = y - mu
            var = _pair_sum(d * d, lo_full) * (1.0 / RWKV_N)
            yn = (d * lax.rsqrt(var + 64e-5)) * lnw_ref[:, sl] + lnb_ref[:, sl]
            o_ref[s, :, sl] = (yn + bonus[p]) * gate[:, sl]

    @pl.when(blk == pl.num_programs(1) - 1)
    def _():
        for s in range(nseq):
            for p in range(npair):
                s_ref[s, p] = st_refs[s * npair + p][...]


def _rwkv(pb, prev, s0_bd, mu, wlr, w0, a0, k_k, k_a, r_k, ln_w, ln_b, tb, chunk):
    bsz, seq, _ = pb.shape
    npair = RWKV_HEADS // 2
    vec = lambda n: pl.BlockSpec((1, n), lambda b, i: (0, 0))
    nseq = RWKV_NSEQ
    nchain = nseq * npair
    scr = lambda: pltpu.VMEM((nseq * tb, RWKV_W), F32)
    nch, c2 = tb // chunk, 2 * chunk
    return pl.pallas_call(
        functools.partial(_rwkv_kernel, nseq, tb, chunk),
        grid=(bsz // nseq, seq // tb),
        in_specs=[
            pl.BlockSpec((nseq, tb, B_COLS), lambda b, i: (b, i, 0)),
            pl.BlockSpec((nseq, 1, B_COLS), lambda b, i: (b, 0, 0)),
            pl.BlockSpec((nseq, npair, LANES, LANES), lambda b, i: (b, 0, 0, 0)),
            vec(B_COLS),
            pl.BlockSpec(wlr.shape, lambda b, i: (0, 0)),
            vec(RWKV_W), vec(RWKV_W), vec(RWKV_W), vec(RWKV_W), vec(RWKV_W), vec(RWKV_W), vec(RWKV_W),
        ],
        out_specs=[
            pl.BlockSpec((nseq, tb, RWKV_W), lambda b, i: (b, i, 0)),
            pl.BlockSpec((nseq, npair, LANES, LANES), lambda b, i: (b, 0, 0, 0)),
        ],
        out_shape=[
            jax.ShapeDtypeStruct((bsz, seq, RWKV_W), F32),
            jax.ShapeDtypeStruct((bsz, npair, LANES, LANES), F32),
        ],
        scratch_shapes=[pltpu.VMEM((LANES, LANES), F32) for _ in range(nchain)] + [
            pltpu.VMEM((nseq, 1, B_COLS), F32),
            scr(), scr(), scr(), scr(), scr(), scr(), scr(), scr(), scr(),
            pltpu.VMEM((nch, nchain, c2, c2), F32), pltpu.VMEM((nch, nchain, c2, c2), F32),
            pltpu.VMEM((nch, nchain, c2, LANES), F32), pltpu.VMEM((nch, nchain, c2, LANES), F32),
        ],
        compiler_params=_cparams(("parallel", "arbitrary")),
        name="rwkv7",
    )(pb, prev, s0_bd, mu.reshape(1, -1), wlr, w0.reshape(1, -1), a0.reshape(1, -1), k_k.reshape(1, -1),
      k_a.reshape(1, -1), r_k.reshape(1, -1), ln_w.reshape(1, -1), ln_b.reshape(1, -1))


def _sb_block(z, mask, carry, tri_u):
    log_w, rem_sum = _sb_block_logs(z, mask, tri_u)
    return _sb_block_weights(log_w, carry, mask), carry + rem_sum


def _sb_block_logs(z, mask, tri_u):
    log_beta = jnp.minimum(z, 0.0) - jnp.log(1.0 + jnp.exp(-jnp.abs(z)))
    log_rem = log_beta - z
    if mask is not None:
        log_rem = jnp.where(mask, log_rem, 0.0)
    hi = log_rem.astype(BF16)
    lo = (log_rem - hi.astype(F32)).astype(BF16)
    after = jnp.dot(jnp.concatenate([hi, lo], axis=1), tri_u, preferred_element_type=F32)
    return log_beta + after, jnp.sum(log_rem, axis=-1, keepdims=True)


def _sb_block_weights(log_w, carry, mask):
    w = jnp.exp(log_w + carry)
    return w if mask is None else jnp.where(mask, w, 0.0)


def _strict_upper(n):
    r = lax.broadcasted_iota(jnp.int32, (2 * n, n), 0) % n
    c = lax.broadcasted_iota(jnp.int32, (2 * n, n), 1)
    return (r > c).astype(BF16)


def _sb_prompt_kernel(tq, tk, q_ref, k_ref, v_ref, o_ref):
    qi = pl.program_id(2)
    q = q_ref[0] * (SB_DH ** -0.5)
    lo = _lo_mask((tq, LANES))
    qa = jnp.where(lo, q, 0.0).astype(BF16)
    qb = jnp.where(lo, 0.0, q).astype(BF16)
    tri_u = _strict_upper(tk)
    qpos = qi * tq + lax.broadcasted_iota(jnp.int32, (tq, tk), 0)
    kofs = lax.broadcasted_iota(jnp.int32, (tq, tk), 1)
    n_diag = tq // tk
    n_full = qi * n_diag

    def visit(kb_idx, state, masked):
        acc_a, acc_b, car_a, car_b = state
        rows = pl.ds(pl.multiple_of(kb_idx * tk, tk), tk)
        kblk = k_ref[0, rows, :].astype(BF16)
        vblk = v_ref[0, rows, :].astype(BF16)
        mask = ((kb_idx * tk + kofs) < qpos) if masked else None
        w_a, car_a = _sb_block(_dot_nt(qa, kblk), mask, car_a, tri_u)
        w_b, car_b = _sb_block(_dot_nt(qb, kblk), mask, car_b, tri_u)
        acc_a = acc_a + jnp.dot(w_a.astype(BF16), vblk, preferred_element_type=F32)
        acc_b = acc_b + jnp.dot(w_b.astype(BF16), vblk, preferred_element_type=F32)
        return acc_a, acc_b, car_a, car_b

    zero = jnp.zeros((tq, LANES), F32)
    zc = jnp.zeros((tq, 1), F32)
    state = (zero, zero, zc, zc)
    for d in reversed(range(n_diag)):
        state = visit(n_full + d, state, True)
    state = lax.fori_loop(0, n_full, lambda j, s: visit(n_full - 1 - j, s, False), state)
    o_ref[0] = jnp.where(lo, state[0], state[1])


def _sb_prompt(q, k, v, tq, tk):
    bsz, seq, _ = q.shape
    npair = SB_HEADS // 2
    return pl.pallas_call(
        functools.partial(_sb_prompt_kernel, tq, tk),
        grid=(bsz, npair, seq // tq),
        in_specs=[
            pl.BlockSpec((1, tq, LANES), lambda b, h, i: (b, i, h)),
            pl.BlockSpec((1, seq, LANES), lambda b, h, i: (b, 0, h)),
            pl.BlockSpec((1, seq, LANES), lambda b, h, i: (b, 0, h)),
        ],
        out_specs=pl.BlockSpec((1, tq, LANES), lambda b, h, i: (b, i, h)),
        out_shape=jax.ShapeDtypeStruct((bsz, seq, SB_W), F32),
        compiler_params=_cparams(("parallel", "parallel", "arbitrary")),
        name="sb_prompt",
    )(q, k, v)


def _sb_sample_kernel(npg, lq, *refs):
    pt_ref = refs[0]
    q_ref, kn_ref, vn_ref = refs[1:4]
    kp_refs = refs[4:4 + npg]
    vp_refs = refs[4 + npg:4 + 2 * npg]
    o_ref = refs[4 + 2 * npg]
    qbd_ref, acc_ref, car_ref = refs[5 + 2 * npg:]
    del pt_ref
    j = pl.program_id(1)
    rows = SB_HEADS * lq
    tri_u = _strict_upper(PAGE)

    def visit_new(kblk, vblk, mask):
        w, car = _sb_block(_dot_nt(qbd_ref[...], kblk), mask, car_ref[...], tri_u)
        car_ref[...] = car
        acc_ref[...] += _dot(w, vblk)

    @pl.when(j == 0)
    def _():
        q = q_ref[0] * (SB_DH ** -0.5)
        qt = jnp.concatenate([q] * SB_HEADS, axis=0)
        rh = lax.broadcasted_iota(jnp.int32, (rows, SB_W), 0) // lq
        ch = lax.broadcasted_iota(jnp.int32, (rows, SB_W), 1) // SB_DH
        qbd_ref[...] = jnp.where(rh == ch, qt, 0.0).astype(BF16)
        acc_ref[...] = jnp.zeros_like(acc_ref)
        car_ref[...] = jnp.zeros_like(car_ref)
        pad = jnp.zeros((PAGE - lq, SB_W), F32)
        kblk = jnp.concatenate([kn_ref[0], pad], axis=0)
        vblk = jnp.concatenate([vn_ref[0], pad], axis=0)
        qidx = lax.broadcasted_iota(jnp.int32, (rows, PAGE), 0) % lq
        kidx = lax.broadcasted_iota(jnp.int32, (rows, PAGE), 1)
        visit_new(kblk, vblk, kidx < qidx)

    qbd = qbd_ref[...]
    logs = [_sb_block_logs(_dot(qbd, kp_ref[0, 0]), None, tri_u) for kp_ref in kp_refs]
    car = car_ref[...]
    ws = []
    for log_w, rem_sum in logs:
        ws.append(_sb_block_weights(log_w, car, None).astype(BF16))
        car = car + rem_sum
    car_ref[...] = car
    v_all = jnp.concatenate([vp_ref[0, 0].astype(BF16) for vp_ref in vp_refs], axis=1)
    acc_ref[...] += _dot_nt(jnp.concatenate(ws, axis=1), v_all)

    @pl.when(j == pl.num_programs(1) - 1)
    def _():
        acc = acc_ref[...]
        ch = lax.broadcasted_iota(jnp.int32, (lq, SB_W), 1) // SB_DH
        out = jnp.zeros((lq, SB_W), F32)
        for h in range(SB_HEADS):
            out = out + jnp.where(ch == h, acc[h * lq:(h + 1) * lq, :], 0.0)
        o_ref[0] = out


def _sb_sample(q, k_new, v_new, cache_k, cache_v, page_table, npg):
    bsz, lq, _ = q.shape
    n_pages = page_table.shape[1]
    rows = SB_HEADS * lq
    tok = pl.BlockSpec((1, lq, SB_W), lambda b, j, pt: (b, 0, 0))

    def page_spec(i):
        return pl.BlockSpec((1, 1, SB_W, PAGE), lambda b, j, pt: (0, pt[b, n_pages - 1 - (j * npg + i)], 0, 0))

    grid_spec = pltpu.PrefetchScalarGridSpec(
        num_scalar_prefetch=1,
        grid=(bsz, n_pages // npg),
        in_specs=[tok, tok, tok] + [page_spec(i) for i in range(npg)] * 2,
        out_specs=pl.BlockSpec((1, lq, SB_W), lambda b, j, pt: (b, 0, 0)),
        scratch_shapes=[
            pltpu.VMEM((rows, SB_W), BF16),
            pltpu.VMEM((rows, SB_W), F32),
            pltpu.VMEM((rows, 1), F32),
        ],
    )
    return pl.pallas_call(
        functools.partial(_sb_sample_kernel, npg, lq),
        grid_spec=grid_spec,
        out_shape=jax.ShapeDtypeStruct((bsz, lq, SB_W), F32),
        compiler_params=_cparams(("parallel", "arbitrary")),
        name="sb_sample",
    )(page_table, q, k_new, v_new, *([cache_k] * npg), *([cache_v] * npg))


def _first_index(cond, idx, big):
    return jnp.min(jnp.where(cond, idx, big), axis=(0, 1), keepdims=True)


def _router_kernel(x_ref, g_ref, sh_ref, sc_ref, rw_ref, rb_ref, h_ref, comb_ref, cnt_ref):
    h = _modulated_norm(x_ref[0], g_ref[...], sh_ref[0], sc_ref[0])
    h_ref[0] = h.astype(BF16)
    tl = h.shape[0]
    logits = _dot_nt(rw_ref[...], h, hp=True)
    scores = jax.nn.sigmoid(logits).reshape(N_GROUPS, GROUP_SIZE, tl)
    biased = scores + rb_ref[...].reshape(N_GROUPS, GROUP_SIZE, 1)
    neg = -jnp.inf

    jidx = lax.broadcasted_iota(jnp.int32, biased.shape, 1)
    m1 = jnp.max(biased, axis=1, keepdims=True)
    first = jnp.min(jnp.where(biased == m1, jidx, GROUP_SIZE), axis=1, keepdims=True)
    m2 = jnp.max(jnp.where(jidx == first, neg, biased), axis=1, keepdims=True)
    gscore = m1 + m2

    gidx = lax.broadcasted_iota(jnp.int32, gscore.shape, 0)
    gsel = jnp.zeros(gscore.shape, jnp.bool_)
    for _ in range(TOPK_GROUPS):
        m = jnp.max(gscore, axis=0, keepdims=True)
        pick = gidx == jnp.min(jnp.where(gscore == m, gidx, N_GROUPS), axis=0, keepdims=True)
        gsel = jnp.logical_or(gsel, pick)
        gscore = jnp.where(pick, neg, gscore)

    cand = jnp.where(gsel, biased, neg)
    eidx = lax.broadcasted_iota(jnp.int32, cand.shape, 0) * GROUP_SIZE + jidx
    esel = jnp.zeros(cand.shape, jnp.bool_)
    for _ in range(TOP_K):
        m = jnp.max(cand, axis=(0, 1), keepdims=True)
        pick = eidx == _first_index(cand == m, eidx, N_EXPERTS)
        esel = jnp.logical_or(esel, pick)
        cand = jnp.where(pick, neg, cand)

    sel = jnp.where(esel, scores, 0.0)
    wts = (sel / jnp.sum(sel, axis=(0, 1), keepdims=True) * ROUTED_SCALE).reshape(N_EXPERTS, tl)
    comb_ref[0] = wts
    cnt = jnp.sum((wts > 0.0).astype(jnp.int32), axis=-1, keepdims=True)
    cnt_ref[0] = jnp.broadcast_to(cnt, (N_EXPERTS, LANES))


def _router(x3, g, sh3, sc3, router_w, router_bias):
    gsz, lg, _ = x3.shape
    tl = MOE_TB
    tpg = lg // tl
    lm = sh3.shape[1]
    tm = 1 if lm == 1 else tl
    mod_map = (lambda b, i: (b, 0, 0)) if lm == 1 else (lambda b, i: (b, i, 0))
    return pl.pallas_call(
        _router_kernel,
        grid=(gsz, tpg),
        in_specs=[
            pl.BlockSpec((1, tl, D_MODEL), lambda b, i: (b, i, 0)),
            pl.BlockSpec((1, D_MODEL), lambda b, i: (0, 0)),
            pl.BlockSpec((1, tm, D_MODEL), mod_map),
            pl.BlockSpec((1, tm, D_MODEL), mod_map),
            pl.BlockSpec((N_EXPERTS, D_MODEL), lambda b, i: (0, 0)),
            pl.BlockSpec((N_EXPERTS, 1), lambda b, i: (0, 0)),
        ],
        out_specs=[
            pl.BlockSpec((1, tl, D_MODEL), lambda b, i: (b, i, 0)),
            pl.BlockSpec((1, N_EXPERTS, tl), lambda b, i: (b * tpg + i, 0, 0)),
            pl.BlockSpec((1, N_EXPERTS, LANES), lambda b, i: (b * tpg + i, 0, 0)),
        ],
        out_shape=[
            jax.ShapeDtypeStruct((gsz, lg, D_MODEL), BF16),
            jax.ShapeDtypeStruct((gsz * tpg, N_EXPERTS, tl), F32),
            jax.ShapeDtypeStruct((gsz * tpg, N_EXPERTS, LANES), jnp.int32),
        ],
        compiler_params=_cparams(("parallel", "parallel")),
        name="moe_router",
    )(x3, g.reshape(1, D_MODEL), sh3, sc3, router_w.T, router_bias.reshape(N_EXPERTS, 1))


def _swiglu_hidden(h, wg, wu):
    return _silu(jnp.dot(h, wg, preferred_element_type=F32)) * jnp.dot(h, wu, preferred_element_type=F32)


def _moe_plan(cnt, nt_max):
    e_ids = N_EXPERTS
    pc = (cnt + MOE_UNIT - 1) // MOE_UNIT * MOE_UNIT
    lend = jnp.cumsum(pc, axis=1)
    lstart = lend - pc
    nun = (lend[:, -1] // MOE_UNIT).astype(jnp.int32)
    ctot = jnp.sum(pc, axis=0)
    rs = (ctot + MOE_RT - 1) // MOE_RT * MOE_RT
    gend = jnp.cumsum(rs)
    gstart = gend - rs
    toff = jnp.cumsum(pc, axis=0) - pc
    urow = jnp.arange(MOE_NU, dtype=jnp.int32) * MOE_UNIT
    eu = jnp.sum((urow[None, :, None] >= lend[:, None, :]).astype(jnp.int32), axis=-1)
    valid = eu < e_ids
    run_start = jnp.max(jnp.where(lstart[:, None, :] <= urow[None, :, None], lstart[:, None, :], 0), axis=-1)
    uinfo = jnp.where(valid, eu * MOE_INFO_SHIFT + (urow[None, :] - run_start), e_ids * MOE_INFO_SHIFT)
    gbase = (gstart[None, :] + toff) // MOE_UNIT
    zn = ((rs - ctot) // MOE_UNIT).astype(jnp.int32)
    zdst = (gstart + ctot)[:, None] // MOE_UNIT + jnp.arange(MOE_RT // MOE_UNIT, dtype=jnp.int32)[None, :]
    n_used = (gend[-1] // MOE_RT).astype(jnp.int32).reshape(1)
    tmap = jnp.sum((jnp.arange(nt_max, dtype=jnp.int32)[:, None] * MOE_RT >= gend[None, :]).astype(jnp.int32), axis=-1)
    tmap = jnp.minimum(tmap, e_ids - 1)
    i32 = lambda a: a.astype(jnp.int32).reshape(-1)
    return dict(uinfo=i32(uinfo), gbase=i32(gbase),nun=nun, zn=zn, zdst=i32(zdst), tmap=i32(tmap), n_used=n_used)


def _routing_rank(sel, rank_s, wsel_s=None):
    tb = sel.shape[1]
    mask = sel > 0.0
    before = (lax.broadcasted_iota(jnp.int32, (tb, tb), 0) < lax.broadcasted_iota(jnp.int32, (tb, tb), 1))
    rank = jnp.dot(mask.astype(BF16), before.astype(BF16), preferred_element_type=F32)
    rank_s[0:N_EXPERTS, :] = jnp.where(mask, rank, -1.0)
    rank_s[N_EXPERTS:, :] = jnp.full((8, tb), -1.0, F32)
    if wsel_s is not None:
        wsel_s[0:N_EXPERTS, :] = sel
        wsel_s[N_EXPERTS:, :] = jnp.zeros((8, tb), F32)


def _unit_rows(uinfo_ref, base, u, rank_s):
    info = uinfo_ref[base + u]
    e = lax.shift_right_logical(info, MOE_INFO_BITS)
    off = jnp.bitwise_and(info, MOE_INFO_SHIFT - 1).astype(F32)
    j = lax.broadcasted_iota(jnp.int32, (MOE_UNIT, MOE_TB), 0).astype(F32)
    return rank_s[pl.ds(e, 1), :] == (j + off), e


def _global_unit(uinfo_ref, gbase_ref, i, u):
    info = uinfo_ref[i * MOE_NU + u]
    e = lax.shift_right_logical(info, MOE_INFO_BITS)
    off = jnp.bitwise_and(info, MOE_INFO_SHIFT - 1)
    return gbase_ref[i * N_EXPERTS + e] + lax.shift_right_logical(off, MOE_UNIT_BITS)


def _unit_copy(src, dst, sem):
    return pltpu.make_async_copy(src, dst, sem)


def _moe_dispatch_kernel(uinfo_ref, gbase_ref, nun_ref, zn_ref, zdst_ref, h_ref, sel_ref, xs_hbm,
                         p_s, xs_s, rank_s, zero_s, sem, zsem):
    i = pl.program_id(0)
    n = nun_ref[i]
    unit = lambda ref, u: ref.at[pl.ds(pl.multiple_of(u * MOE_UNIT, MOE_UNIT), MOE_UNIT)]

    @pl.when(i == 0)
    def _():
        zero_s[...] = jnp.zeros_like(zero_s)
        zper = MOE_RT // MOE_UNIT

        def per_expert(e, tot):
            def one(z, c):
                _unit_copy(zero_s, unit(xs_hbm, zdst_ref[e * zper + z]), zsem).start()
                return c
            lax.fori_loop(0, zn_ref[e], one, 0)
            return tot + zn_ref[e]

        total = lax.fori_loop(0, N_EXPERTS, per_expert, 0)

        def wait_one(z, c):
            _unit_copy(zero_s, unit(xs_hbm, 0), zsem).wait()
            return c
        lax.fori_loop(0, total, wait_one, 0)

    _routing_rank(sel_ref[0], rank_s)
    base = i * MOE_NU
    for u in range(MOE_NU):
        hit, _ = _unit_rows(uinfo_ref, base, u, rank_s)
        p_s[u * MOE_UNIT:(u + 1) * MOE_UNIT, :] = hit.astype(BF16)

    h = h_ref[0]
    for c in range(MOE_RMAX // MOE_CH):
        @pl.when(c * (MOE_CH // MOE_UNIT) < n)
        def _():
            rows = slice(c * MOE_CH, (c + 1) * MOE_CH)
            xs_s[rows, :] = jnp.dot(p_s[rows, :], h, preferred_element_type=F32).astype(BF16)

    def issue(u, c):
        _unit_copy(unit(xs_s, u), unit(xs_hbm, _global_unit(uinfo_ref, gbase_ref, i, u)), sem).start()
        return c
    lax.fori_loop(0, n, issue, 0)

    def wait_one(u, c):
        _unit_copy(unit(xs_s, 0), unit(xs_hbm, 0), sem).wait()
        return c
    lax.fori_loop(0, n, wait_one, 0)


def _moe_dispatch(h3, sel, plan, pmax):
    gsz, lg, _ = h3.shape
    tpg = lg // MOE_TB
    n_tiles = gsz * tpg
    grid_spec = pltpu.PrefetchScalarGridSpec(
        num_scalar_prefetch=5,
        grid=(n_tiles,),
        in_specs=[
            pl.BlockSpec((1, MOE_TB, D_MODEL), lambda i, *_: (i // tpg, i % tpg, 0)),
            pl.BlockSpec((1, N_EXPERTS, MOE_TB), lambda i, *_: (i, 0, 0)),
        ],
        out_specs=pl.BlockSpec(memory_space=pl.ANY),
        scratch_shapes=[
            pltpu.VMEM((MOE_RMAX, MOE_TB), BF16),
            pltpu.VMEM((MOE_RMAX, D_MODEL), BF16),
            pltpu.VMEM((N_EXPERTS + 8, MOE_TB), F32),
            pltpu.VMEM((MOE_UNIT, D_MODEL), BF16),
            pltpu.SemaphoreType.DMA(()),
            pltpu.SemaphoreType.DMA(()),
        ],
    )
    return pl.pallas_call(
        _moe_dispatch_kernel,
        grid_spec=grid_spec,
        out_shape=jax.ShapeDtypeStruct((pmax, D_MODEL), BF16),
        compiler_params=_cparams(("arbitrary",)),
        name="moe_dispatch",
    )(plan['uinfo'], plan['gbase'], plan['nun'], plan['zn'], plan['zdst'], h3, sel)


def _moe_group_kernel(tmap_ref, nused_ref, xs_ref, wg_ref, wu_ref, wd_ref, ys_ref, wg_s, wu_s, wd_s):
    j = pl.program_id(0)

    @pl.when(j < nused_ref[0])
    def _():
        @pl.when(jnp.logical_or(j == 0, tmap_ref[j] != tmap_ref[jnp.maximum(j - 1, 0)]))
        def _():
            wg_s[...] = wg_ref[0, 0].astype(BF16)
            wu_s[...] = wu_ref[0, 0].astype(BF16)
            wd_s[...] = wd_ref[0, 0].astype(BF16)

        act = _swiglu_hidden(xs_ref[...], wg_s[...], wu_s[...])
        ys_ref[...] = jnp.dot(act.astype(BF16), wd_s[...], preferred_element_type=F32).astype(BF16)


def _moe_group(xs, plan, wg, wu, wd, layer):
    pmax = xs.shape[0]
    nt = pmax // MOE_RT
    row_map = lambda j, tmap, nused: (jnp.maximum(jnp.minimum(j, nused[0] - 1), 0), 0)
    wmap = lambda j, tmap, nused: (layer, tmap[j], 0, 0)
    grid_spec = pltpu.PrefetchScalarGridSpec(
        num_scalar_prefetch=2,
        grid=(nt,),
        in_specs=[
            pl.BlockSpec((MOE_RT, D_MODEL), row_map),
            pl.BlockSpec((1, 1, D_MODEL, HIDDEN), wmap),
            pl.BlockSpec((1, 1, D_MODEL, HIDDEN), wmap),
            pl.BlockSpec((1, 1, HIDDEN, D_MODEL), wmap),
        ],
        out_specs=pl.BlockSpec((MOE_RT, D_MODEL), row_map),
        scratch_shapes=[
            pltpu.VMEM((D_MODEL, HIDDEN), BF16),
            pltpu.VMEM((D_MODEL, HIDDEN), BF16),
            pltpu.VMEM((HIDDEN, D_MODEL), BF16),
        ],
    )
    return pl.pallas_call(
        _moe_group_kernel,
        grid_spec=grid_spec,
        out_shape=jax.ShapeDtypeStruct((pmax, D_MODEL), BF16),
        compiler_params=_cparams(("arbitrary",)),
        name="moe_group",
    )(plan['tmap'], plan['n_used'], xs, wg, wu, wd)


def _moe_combine_kernel(final, uinfo_ref, gbase_ref, nun_ref, h_ref, sel_ref, x_ref, gate_ref, fn_ref,
                        sg_ref, su_ref, sd_ref, ys_hbm, o_ref, pw_s, ys_s, rank_s, wsel_s, acc_s, sem):
    i = pl.program_id(0)
    n = nun_ref[i]
    base = i * MOE_NU
    unit = lambda ref, u: ref.at[pl.ds(pl.multiple_of(u * MOE_UNIT, MOE_UNIT), MOE_UNIT)]

    @pl.when(i == 0)
    def _():
        ys_s[...] = jnp.zeros_like(ys_s)

    def issue(u, c):
        _unit_copy(unit(ys_hbm, _global_unit(uinfo_ref, gbase_ref, i, u)), unit(ys_s, u), sem).start()
        return c
    lax.fori_loop(0, n, issue, 0)

    _routing_rank(sel_ref[0], rank_s, wsel_s)
    for u in range(MOE_NU):
        hit, e = _unit_rows(uinfo_ref, base, u, rank_s)
        pw_s[u * MOE_UNIT:(u + 1) * MOE_UNIT, :] = jnp.where(hit, wsel_s[pl.ds(e, 1), :], 0.0).astype(BF16)

    h = h_ref[0]
    act = _swiglu_hidden(h, sg_ref[...], su_ref[...])
    acc_s[...] = jnp.dot(act.astype(BF16), sd_ref[...], preferred_element_type=F32)

    def wait_one(u, c):
        _unit_copy(unit(ys_hbm, 0), unit(ys_s, 0), sem).wait()
        return c
    lax.fori_loop(0, n, wait_one, 0)

    for c in range(MOE_RMAX // MOE_CH):
        @pl.when(c * (MOE_CH // MOE_UNIT) < n)
        def _():
            rows = slice(c * MOE_CH, (c + 1) * MOE_CH)
            acc_s[...] += _dot_tn(pw_s[rows, :], ys_s[rows, :])

    y = x_ref[0] + gate_ref[0] * acc_s[...]
    if final:
        y = (y * lax.rsqrt(jnp.mean(y * y, axis=-1, keepdims=True) + EPS)) * fn_ref[...]
    o_ref[0] = y


def _moe_combine(x3, gate3, h3, sel, ys, plan, sg, su, sd, final_norm, final):
    gsz, lg, _ = x3.shape
    tpg = lg // MOE_TB
    lm = gate3.shape[1]
    tm = 1 if lm == 1 else MOE_TB
    tok_map = lambda i, *_: (i // tpg, i % tpg, 0)
    mod_map = (lambda i, *_: (i // tpg, 0, 0)) if lm == 1 else tok_map
    full = lambda a: pl.BlockSpec(a.shape, lambda i, *_: (0, 0))
    grid_spec = pltpu.PrefetchScalarGridSpec(
        num_scalar_prefetch=3,
        grid=(gsz * tpg,),
        in_specs=[
            pl.BlockSpec((1, MOE_TB, D_MODEL), tok_map),
            pl.BlockSpec((1, N_EXPERTS, MOE_TB), lambda i, *_: (i, 0, 0)),
            pl.BlockSpec((1, MOE_TB, D_MODEL), tok_map),
            pl.BlockSpec((1, tm, D_MODEL), mod_map),
            pl.BlockSpec((1, D_MODEL), lambda i, *_: (0, 0)),
            full(sg), full(su), full(sd),
            pl.BlockSpec(memory_space=pl.ANY),
        ],
        out_specs=pl.BlockSpec((1, MOE_TB, D_MODEL), tok_map),
        scratch_shapes=[
            pltpu.VMEM((MOE_RMAX, MOE_TB), BF16),
            pltpu.VMEM((MOE_RMAX, D_MODEL), BF16),
            pltpu.VMEM((N_EXPERTS + 8, MOE_TB), F32),
            pltpu.VMEM((N_EXPERTS + 8, MOE_TB), F32),
            pltpu.VMEM((MOE_TB, D_MODEL), F32),
            pltpu.SemaphoreType.DMA(()),
        ],
    )
    return pl.pallas_call(
        functools.partial(_moe_combine_kernel, final),
        grid_spec=grid_spec,
        out_shape=jax.ShapeDtypeStruct(x3.shape, F32),
        compiler_params=_cparams(("arbitrary",)),
        name="moe_combine",
    )(plan['uinfo'], plan['gbase'], plan['nun'], h3, sel, x3, gate3, final_norm.reshape(1, D_MODEL), sg, su, sd, ys)


def _block_diag_pairs(s):
    bsz, nh, n, _ = s.shape
    s = s.reshape(bsz, nh // 2, 2, n, n)
    z = jnp.zeros_like(s[:, :, 0])
    top = jnp.concatenate([s[:, :, 0], z], axis=-1)
    bot = jnp.concatenate([z, s[:, :, 1]], axis=-1)
    return jnp.concatenate([top, bot], axis=-2)


def _unpair(s_bd):
    bsz, npair, n2, _ = s_bd.shape
    n = n2 // 2
    return jnp.stack([s_bd[:, :, :n, :n], s_bd[:, :, n:, n:]], axis=2).reshape(bsz, npair * 2, n, n)


def _rotary_tables(pos):
    half = RET_DK // 2
    inv = ROPE_BASE ** (-jnp.arange(half, dtype=F32) / half)
    ang = pos.astype(F32)[:, None] * inv[None, :]
    cos, sin = jnp.cos(ang), jnp.sin(ang)
    cos_t = jnp.concatenate([cos, cos, cos, cos], axis=-1)
    sin_t = jnp.concatenate([-sin, sin, -sin, sin], axis=-1)
    return cos_t, sin_t


def _trunk(x, c, pos, ret_s, rwkv_s, shift_s, cache, p, flat):
    bsz, seq, _ = x.shape
    mod = _ada(c, p['ada_w'], p['ada_b'])
    mod = mod.reshape(DEPTH, bsz, 6, D_MODEL)
    if flat:
        tok = lambda t: t.reshape(1, bsz * seq, t.shape[-1])
        untok = lambda t: t.reshape(bsz, seq, t.shape[-1])
        modv = lambda l, j: jnp.broadcast_to(mod[l, :, j][:, None, :], (bsz, seq, D_MODEL)).reshape(1, bsz * seq, D_MODEL)
        tl_proj = bsz * seq
    else:
        tok = untok = lambda t: t
        modv = lambda l, j: mod[l, :, j][:, None, :]
        tl_proj = min(256, seq)

    x3 = tok(x)
    w_in = p['w_in_ab'][0].astype(BF16)
    a_cols = 4 * RET_W
    zeros_a = jnp.zeros((1, a_cols), F32)
    zeros_b = jnp.zeros((1, B_COLS), F32)
    pa, pb = _norm_proj(x3, p['norm_mix'][0], modv(0, 0), modv(0, 1), [w_in[:, :a_cols], w_in[:, a_cols:]],
                        [zeros_a, zeros_b], tl_proj)
    pa, pb = untok(pa), untok(pb)
    cos_t, sin_t = _rotary_tables(pos)
    lgs = jnp.log1p(-jnp.exp2(-5.0 - jnp.arange(RET_HEADS, dtype=F32)))
    ret_chunk = math.gcd(seq, 128)
    o_a, ret_new = _retention(pa, cos_t, sin_t, lgs, p['ret_gn_w'][0], p['ret_gn_b'][0],
                              _block_diag_pairs(ret_s), ret_chunk)
    wlr = jnp.zeros((LOWRANK, 3 * RWKV_W), F32)
    wlr = wlr.at[0:64, 0:RWKV_W].set(p['rwkv_w_up'][0])
    wlr = wlr.at[64:128, RWKV_W:2 * RWKV_W].set(p['rwkv_a_up'][0])
    wlr = wlr.at[128:256, 2 * RWKV_W:].set(p['rwkv_g_up'][0])
    tb = min(seq, 128)
    o_b, rwkv_new = _rwkv(pb, shift_s[:, None, :], _block_diag_pairs(rwkv_s), p['rwkv_mu'][0], wlr.astype(BF16),
                          p['rwkv_w0'][0], p['rwkv_a0'][0], p['rwkv_k_k'][0], p['rwkv_k_a'][0],
                          p['rwkv_r_k'][0].reshape(-1), p['rwkv_ln_w'][0], p['rwkv_ln_b'][0],
                          tb, min(RWKV_CHUNK, seq))
    w_out = p['w_out_ab'][0].astype(BF16)
    x3 = _out_proj(x3, modv(0, 2), [tok(o_a), tok(o_b)], [w_out[:RET_W], w_out[RET_W:]], tl_proj)
    x3 = _moe_layer(x3, 0, modv, p, final=False)

    wq, wk, wv = jnp.split(p['w_qkv_c'][0].astype(BF16), 3, axis=1)
    q, k, v = _norm_proj(x3, p['norm_mix'][1], modv(1, 0), modv(1, 1), [wq, wk, wv],
                         [p['b_q_c'][0][None, :], p['b_k_c'][0][None, :], jnp.zeros((1, SB_W), F32)], tl_proj)
    q, k, v = untok(q), untok(k), untok(v)
    if cache is None:
        o = _sb_prompt(q, k, v, min(512, seq), min(256, seq))
    else:
        cache_k, cache_v, page_table = cache
        o = _sb_sample(q, k, v, cache_k, cache_v, page_table, 8)
    x3 = _out_proj(x3, modv(1, 2), [tok(o)], [p['w_out_c'][0].astype(BF16)], tl_proj)
    y3 = _moe_layer(x3, 1, modv, p, final=True)

    kv_shape = (1, bsz, seq, SB_HEADS, SB_DH)
    return (untok(y3), _unpair(ret_new)[None], _unpair(rwkv_new)[None], pb[:, -1][None],
            k.reshape(kv_shape), v.reshape(kv_shape))


def _moe_layer(x3, l, modv, p, final):
    gsz, lg, _ = x3.shape
    n_tiles = gsz * lg // MOE_TB
    pmax = gsz * lg * TOP_K + n_tiles * N_EXPERTS * (MOE_UNIT - 1) + N_EXPERTS * (MOE_RT - MOE_UNIT)
    pmax = -(-pmax // MOE_RT) * MOE_RT
    h3, sel, cnt = _router(x3, p['norm_ffn'][l], modv(l, 3), modv(l, 4), p['router_w'][l], p['router_bias'][l])
    plan = _moe_plan(cnt[:, :, 0], pmax // MOE_RT)
    xs = _moe_dispatch(h3, sel, plan, pmax)
    ys = _moe_group(xs, plan, p['exp_w_gate'], p['exp_w_up'], p['exp_w_down'], l)
    return _moe_combine(x3, modv(l, 5), h3, sel, ys, plan, p['sh_w_gate'][l].astype(BF16),
                        p['sh_w_up'][l].astype(BF16), p['sh_w_down'][l].astype(BF16), p['final_norm'], final)


def kernel(x_prompt, x_sample, c_prompt, c_sample, state_ret, state_rwkv, state_shift, cache_k, cache_v, page_table, ada_w, ada_b, norm_mix, norm_ffn, final_norm, w_in_ab, w_out_ab, ret_gn_w, ret_gn_b, rwkv_mu, rwkv_w0, rwkv_w_up, rwkv_a0, rwkv_a_up, rwkv_g_up, rwkv_k_k, rwkv_k_a, rwkv_r_k, rwkv_ln_w, rwkv_ln_b, w_qkv_c, b_q_c, b_k_c, w_out_c, router_w, router_bias, exp_w_gate, exp_w_up, exp_w_down, sh_w_gate, sh_w_up, sh_w_down):
    p = dict(ada_w=ada_w, ada_b=ada_b, norm_mix=norm_mix, norm_ffn=norm_ffn, final_norm=final_norm,
             w_in_ab=w_in_ab, w_out_ab=w_out_ab, ret_gn_w=ret_gn_w, ret_gn_b=ret_gn_b, rwkv_mu=rwkv_mu,
             rwkv_w0=rwkv_w0, rwkv_w_up=rwkv_w_up, rwkv_a0=rwkv_a0, rwkv_a_up=rwkv_a_up, rwkv_g_up=rwkv_g_up,
             rwkv_k_k=rwkv_k_k, rwkv_k_a=rwkv_k_a, rwkv_r_k=rwkv_r_k, rwkv_ln_w=rwkv_ln_w, rwkv_ln_b=rwkv_ln_b,
             w_qkv_c=w_qkv_c, b_q_c=b_q_c, b_k_c=b_k_c, w_out_c=w_out_c, router_w=router_w,
             router_bias=router_bias, exp_w_gate=exp_w_gate, exp_w_up=exp_w_up, exp_w_down=exp_w_down,
             sh_w_gate=sh_w_gate, sh_w_up=sh_w_up, sh_w_down=sh_w_down)
    bp, lp, _ = x_prompt.shape
    bs, ls, _ = x_sample.shape
    zeros = lambda *s: jnp.zeros(s, F32)
    y_p, ret_p, rwkv_p, shift_p, k_p, v_p = _trunk(
        x_prompt, c_prompt, jnp.arange(lp), zeros(bp, RET_HEADS, RET_DK, RET_DK),
        zeros(bp, RWKV_HEADS, RWKV_N, RWKV_N), zeros(bp, B_COLS), None, p, flat=False)
    n_pages = page_table.shape[1]
    n_pool = cache_k.shape[1]
    pages = lambda t: jnp.transpose(t, (0, 1, 3, 4, 2)).reshape(1, n_pool, SB_W, PAGE)
    cache = (pages(cache_k), pages(cache_v), page_table)
    y_s, ret_s, rwkv_s, shift_s, k_s, v_s = _trunk(
        x_sample, c_sample, n_pages * PAGE + jnp.arange(ls), state_ret[0], state_rwkv[0], state_shift[0],
        cache, p, flat=True)
    return (y_p, y_s, ret_p, ret_s, rwkv_p, rwkv_s, shift_p, shift_s, k_p, v_p, k_s, v_s)
```

```python
import functools
import math

import jax
import jax.numpy as jnp
from jax import lax
from jax.experimental import pallas as pl
from jax.experimental.pallas import tpu as pltpu

F32 = jnp.float32
BF16 = jnp.bfloat16
HIGHEST = lax.Precision.HIGHEST

D_MODEL = 1024
DEPTH = 2
PAGE = 128
RET_HEADS = 8
RET_DK = 64
RWKV_HEADS = 8
RWKV_N = 64
RWKV_W = RWKV_HEADS * RWKV_N
RET_W = RET_HEADS * RET_DK
LOWRANK = 256
B_COLS = 3 * RWKV_W + LOWRANK
SB_HEADS = 16
SB_DH = 64
SB_W = SB_HEADS * SB_DH
N_EXPERTS = 64
N_GROUPS = 8
GROUP_SIZE = N_EXPERTS // N_GROUPS
TOPK_GROUPS = 4
TOP_K = 8
HIDDEN = 256
ROUTED_SCALE = 2.5
EPS = 1e-6
ROPE_BASE = 10000.0
LANES = 128
VMEM_LIMIT = 56 * 1024 * 1024

MOE_TB = 256
MOE_UNIT_BITS = 4
MOE_UNIT = 1 << MOE_UNIT_BITS
MOE_RT = 512
MOE_CH = 512
MOE_RMAX = -(-(MOE_TB * TOP_K + N_EXPERTS * (MOE_UNIT - 1)) // MOE_CH) * MOE_CH
MOE_NU = MOE_RMAX // MOE_UNIT
MOE_ISSUE_BITS = 3
MOE_ISSUE = 1 << MOE_ISSUE_BITS
MOE_INFO_BITS = 9
MOE_INFO_SHIFT = 1 << MOE_INFO_BITS

RWKV_CHUNK = 16
RWKV_NSEQ = 4
RWKV_HP_LOCAL = False
RWKV_HP_STATE = False


def _cparams(sem):
    return pltpu.CompilerParams(dimension_semantics=sem, vmem_limit_bytes=VMEM_LIMIT)


def _dot(a, b, hp=False):
    if hp:
        return jnp.dot(a.astype(F32), b.astype(F32), preferred_element_type=F32, precision=HIGHEST)
    return jnp.dot(a.astype(BF16), b.astype(BF16), preferred_element_type=F32)


def _dot_nt(a, b, hp=False):
    dn = (((1,), (1,)), ((), ()))
    if hp:
        return lax.dot_general(a.astype(F32), b.astype(F32), dn, preferred_element_type=F32, precision=HIGHEST)
    return lax.dot_general(a.astype(BF16), b.astype(BF16), dn, preferred_element_type=F32)


def _dot_tn(a, b, hp=False):
    dn = (((0,), (0,)), ((), ()))
    if hp:
        return lax.dot_general(a.astype(F32), b.astype(F32), dn, preferred_element_type=F32, precision=HIGHEST)
    return lax.dot_general(a.astype(BF16), b.astype(BF16), dn, preferred_element_type=F32)


def _split_bf16(x):
    hi = x.astype(BF16)
    return hi, x - hi.astype(F32)


def _dot_x3(a, b):
    ah, al = _split_bf16(a)
    bh, bl = _split_bf16(b)
    d = lambda x, y: jnp.dot(x, y.astype(BF16), preferred_element_type=F32)
    return d(ah, bh) + (d(ah, bl) + d(al.astype(BF16), bh))


def _dot_exact_lhs(a, b):
    a = a.astype(BF16)
    b1, r1 = _split_bf16(b)
    b2, r2 = _split_bf16(r1)
    d = lambda y: jnp.dot(a, y.astype(BF16), preferred_element_type=F32)
    return d(b1) + (d(b2) + d(r2))


def _silu(x):
    return x * jax.nn.sigmoid(x)


def _lo_mask(shape):
    return (lax.broadcasted_iota(jnp.int32, shape, len(shape) - 1) % LANES) < (LANES // 2)


def _pair_sum(x, lo):
    s_lo = jnp.sum(jnp.where(lo, x, 0.0), axis=-1, keepdims=True)
    s_hi = jnp.sum(jnp.where(lo, 0.0, x), axis=-1, keepdims=True)
    return jnp.where(lo, s_lo, s_hi)


def _ada_kernel(c_ref, w_ref, b_ref, o_ref):
    o_ref[0] = _dot(_silu(c_ref[...]), w_ref[0]) + b_ref[0]


def _ada(c, ada_w, ada_b):
    bsz = c.shape[0]
    n = ada_w.shape[-1]
    tn = D_MODEL
    return pl.pallas_call(
        _ada_kernel,
        grid=(DEPTH, n // tn),
        in_specs=[
            pl.BlockSpec((bsz, D_MODEL), lambda l, j: (0, 0)),
            pl.BlockSpec((1, D_MODEL, tn), lambda l, j: (l, 0, j)),
            pl.BlockSpec((1, 1, tn), lambda l, j: (l, 0, j)),
        ],
        out_specs=pl.BlockSpec((1, bsz, tn), lambda l, j: (l, 0, j)),
        out_shape=jax.ShapeDtypeStruct((DEPTH, bsz, n), F32),
        compiler_params=_cparams(("parallel", "parallel")),
        name="ada_mod",
    )(c, ada_w, ada_b.reshape(DEPTH, 1, n))


def _modulated_norm(x, g, sh, sc):
    y = x * lax.rsqrt(jnp.mean(x * x, axis=-1, keepdims=True) + EPS)
    return (y * g) * (1.0 + sc) + sh


def _norm_proj_kernel(n_out, x_ref, g_ref, sh_ref, sc_ref, *refs):
    w_refs, b_refs, o_refs = refs[:n_out], refs[n_out:2 * n_out], refs[2 * n_out:]
    h = _modulated_norm(x_ref[0], g_ref[...], sh_ref[0], sc_ref[0]).astype(BF16)
    for w_ref, b_ref, o_ref in zip(w_refs, b_refs, o_refs):
        o_ref[0] = jnp.dot(h, w_ref[...], preferred_element_type=F32) + b_ref[...]


def _norm_proj(x3, g, sh3, sc3, ws, bs, tl):
    gsz, lg, _ = x3.shape
    lm = sh3.shape[1]
    tm = 1 if lm == 1 else tl
    mod_map = (lambda b, i: (b, 0, 0)) if lm == 1 else (lambda b, i: (b, i, 0))
    n_out = len(ws)
    in_specs = [
        pl.BlockSpec((1, tl, D_MODEL), lambda b, i: (b, i, 0)),
        pl.BlockSpec((1, D_MODEL), lambda b, i: (0, 0)),
        pl.BlockSpec((1, tm, D_MODEL), mod_map),
        pl.BlockSpec((1, tm, D_MODEL), mod_map),
    ]
    in_specs += [pl.BlockSpec(w.shape, lambda b, i: (0, 0)) for w in ws]
    in_specs += [pl.BlockSpec(b.shape, lambda b, i: (0, 0)) for b in bs]
    return pl.pallas_call(
        functools.partial(_norm_proj_kernel, n_out),
        grid=(gsz, lg // tl),
        in_specs=in_specs,
        out_specs=[pl.BlockSpec((1, tl, w.shape[1]), lambda b, i: (b, i, 0)) for w in ws],
        out_shape=[jax.ShapeDtypeStruct((gsz, lg, w.shape[1]), F32) for w in ws],
        compiler_params=_cparams(("parallel", "parallel")),
        name="norm_proj",
    )(x3, g.reshape(1, D_MODEL), sh3, sc3, *ws, *bs)


def _out_proj_kernel(n_in, x_ref, gate_ref, *refs):
    o_refs, w_refs, y_ref = refs[:n_in], refs[n_in:2 * n_in], refs[2 * n_in]
    acc = None
    for o_ref, w_ref in zip(o_refs, w_refs):
        t = jnp.dot(o_ref[0].astype(BF16), w_ref[...], preferred_element_type=F32)
        acc = t if acc is None else acc + t
    y_ref[0] = x_ref[0] + gate_ref[0] * acc


def _out_proj(x3, gate3, os_, ws, tl):
    gsz, lg, _ = x3.shape
    lm = gate3.shape[1]
    tm = 1 if lm == 1 else tl
    mod_map = (lambda b, i: (b, 0, 0)) if lm == 1 else (lambda b, i: (b, i, 0))
    n_in = len(os_)
    in_specs = [
        pl.BlockSpec((1, tl, D_MODEL), lambda b, i: (b, i, 0)),
        pl.BlockSpec((1, tm, D_MODEL), mod_map),
    ]
    in_specs += [pl.BlockSpec((1, tl, o.shape[-1]), lambda b, i: (b, i, 0)) for o in os_]
    in_specs += [pl.BlockSpec(w.shape, lambda b, i: (0, 0)) for w in ws]
    return pl.pallas_call(
        functools.partial(_out_proj_kernel, n_in),
        grid=(gsz, lg // tl),
        in_specs=in_specs,
        out_specs=pl.BlockSpec((1, tl, D_MODEL), lambda b, i: (b, i, 0)),
        out_shape=jax.ShapeDtypeStruct(x3.shape, F32),
        compiler_params=_cparams(("parallel", "parallel")),
        name="out_proj",
    )(x3, gate3, *os_, *ws)


def _rot_half(x, lo32):
    return jnp.where(lo32, pltpu.roll(x, LANES - 32, 1), pltpu.roll(x, 32, 1))


def _retention_kernel(chunk, lgs_ref, q_ref, k_ref, v_ref, g_ref, cos_ref, sin_ref, gw_ref, gb_ref, s0_ref,
                      o_ref, s_ref, st_ref):
    npair = RET_HEADS // 2
    c = pl.program_id(1)

    @pl.when(c == 0)
    def _():
        st_ref[...] = s0_ref[0]

    lane = lax.broadcasted_iota(jnp.int32, (chunk, LANES), 1)
    lo = lane < (LANES // 2)
    lo32 = (lane % RET_DK) < (RET_DK // 2)
    cos = cos_ref[...]
    sin = sin_ref[...]
    ti = lax.broadcasted_iota(jnp.int32, (chunk, LANES), 0).astype(F32)
    ii = lax.broadcasted_iota(jnp.int32, (chunk, chunk), 0)
    jj = lax.broadcasted_iota(jnp.int32, (chunk, chunk), 1)
    diff = jnp.maximum(ii - jj, 0).astype(F32)
    causal = ii >= jj
    ri = lax.broadcasted_iota(jnp.int32, (LANES, LANES), 0)
    ci = lax.broadcasted_iota(jnp.int32, (LANES, LANES), 1)
    same_head = (ri < RET_DK) == (ci < RET_DK)

    def pair(p, st):
        sl = slice(p * LANES, (p + 1) * LANES)
        lg_a = lgs_ref[2 * p]
        lg_b = lgs_ref[2 * p + 1]
        lg = jnp.where(lo, lg_a, lg_b)
        q = q_ref[0, :, sl]
        k = k_ref[0, :, sl]
        q = q * cos + _rot_half(q, lo32) * sin
        k = (k * cos + _rot_half(k, lo32) * sin) * (RET_DK ** -0.5)
        kb = k.astype(BF16)
        vb = v_ref[0, :, sl].astype(BF16)

        def head(lg_h, sel):
            dmask = jnp.where(causal, jnp.exp(lg_h * diff), 0.0)
            sc = _dot_nt(jnp.where(sel, q, 0.0), kb) * dmask
            return _dot(sc, vb)

        o = jnp.where(lo, head(lg_a, lo), head(lg_b, jnp.logical_not(lo)))
        o = o + _dot(q * jnp.exp(lg * (ti + 1.0)), st)
        c_dec = jnp.exp(jnp.where(ri < RET_DK, lg_a, lg_b) * float(chunk))
        st_new = st * c_dec + jnp.where(same_head, _dot_tn(k * jnp.exp(lg * (chunk - 1.0 - ti)), vb), 0.0)
        mu = _pair_sum(o, lo) * (1.0 / RET_DK)
        d = o - mu
        var = _pair_sum(d * d, lo) * (1.0 / RET_DK)
        y = (d * lax.rsqrt(var + 1e-5)) * gw_ref[:, sl] + gb_ref[:, sl]
        return y * _silu(g_ref[0, :, sl]), st_new

    states = [st_ref[p] for p in range(npair)]
    results = [pair(p, states[p]) for p in range(npair)]
    for p, (y, st_new) in enumerate(results):
        o_ref[0, :, p * LANES:(p + 1) * LANES] = y
        st_ref[p] = st_new

    @pl.when(c == pl.num_programs(1) - 1)
    def _():
        for p, (_, st_new) in enumerate(results):
            s_ref[0, p] = st_new


def _retention(pa, cos_t, sin_t, lgs, gn_w, gn_b, s0_bd, chunk):
    bsz, seq, _ = pa.shape
    npair = RET_HEADS // 2
    nc = seq // chunk
    col = lambda j: pl.BlockSpec((1, chunk, RET_W), lambda b, c: (b, c, j))
    return pl.pallas_call(
        functools.partial(_retention_kernel, chunk),
        grid=(bsz, nc),
        in_specs=[
            pl.BlockSpec(memory_space=pltpu.SMEM),
            col(0), col(1), col(2), col(3),
            pl.BlockSpec((chunk, LANES), lambda b, c: (c, 0)),
            pl.BlockSpec((chunk, LANES), lambda b, c: (c, 0)),
            pl.BlockSpec((1, RET_W), lambda b, c: (0, 0)),
            pl.BlockSpec((1, RET_W), lambda b, c: (0, 0)),
            pl.BlockSpec((1, npair, LANES, LANES), lambda b, c: (b, 0, 0, 0)),
        ],
        out_specs=[
            pl.BlockSpec((1, chunk, RET_W), lambda b, c: (b, c, 0)),
            pl.BlockSpec((1, npair, LANES, LANES), lambda b, c: (b, 0, 0, 0)),
        ],
        out_shape=[
            jax.ShapeDtypeStruct((bsz, seq, RET_W), F32),
            jax.ShapeDtypeStruct((bsz, npair, LANES, LANES), F32),
        ],
        scratch_shapes=[pltpu.VMEM((npair, LANES, LANES), F32)],
        compiler_params=_cparams(("parallel", "arbitrary")),
        name="retention",
    )(lgs, pa, pa, pa, pa, cos_t, sin_t, gn_w.reshape(1, RET_W), gn_b.reshape(1, RET_W), s0_bd)


def _rwkv_kernel(nseq, tb, chunk, pb_ref, prev_ref, s0_ref, mu_ref, wlr_ref, w0_ref, a0_ref, kk_ref, ka_ref, rk_ref,
                 lnw_ref, lnb_ref, o_ref, s_ref, *scratch):
    npair = RWKV_HEADS // 2
    nchain = nseq * npair
    st_refs = scratch[:nchain]
    carry_ref, a_s, b_s, k_s, r_s, v_s, bp_s, kp_s, pc_s, y_s, tinv_s, lrb_s, lva_s, lvr_s = scratch[nchain:]
    blk = pl.program_id(1)

    @pl.when(blk == 0)
    def _():
        for s in range(nseq):
            for p in range(npair):
                st_refs[s * npair + p][...] = s0_ref[s, p]
            carry_ref[s] = prev_ref[s]

    lo_full = _lo_mask((tb, LANES))
    ti = lax.broadcasted_iota(jnp.int32, (tb, tb), 0)
    tj = lax.broadcasted_iota(jnp.int32, (tb, tb), 1)
    same_chunk = (ti // chunk) == (tj // chunk)
    sel = jnp.concatenate([jnp.logical_and(same_chunk, ti >= tj), same_chunk], axis=0).astype(F32)

    def prepare(s):
        rs = slice(s * tb, (s + 1) * tb)
        pb = pb_ref[s]
        row = lax.broadcasted_iota(jnp.int32, pb.shape, 0)
        prev = jnp.where(row == 0, carry_ref[s], pltpu.roll(pb, 1, 0))
        carry_ref[s] = pb[tb - 1:tb, :]
        pbs = pb + (prev - pb) * mu_ref[...]
        r = pbs[:, :RWKV_W]
        kb = pbs[:, RWKV_W:2 * RWKV_W]
        v = pbs[:, 2 * RWKV_W:3 * RWKV_W]
        tail = pbs[:, 3 * RWKV_W:]
        tl_lane = lax.broadcasted_iota(jnp.int32, tail.shape, 1)
        act = jnp.where(tl_lane < 64, jnp.tanh(tail), jnp.where(tl_lane < 128, tail, jax.nn.sigmoid(tail)))
        lr = _dot(act, wlr_ref[...])
        wz = -(w0_ref[...] + lr[:, :RWKV_W])
        softplus = jnp.maximum(wz, 0.0) + jnp.log1p(jnp.exp(-jnp.abs(wz)))
        logw = -jnp.exp(-softplus - 0.5)
        a = jax.nn.sigmoid(a0_ref[...] + lr[:, RWKV_W:2 * RWKV_W])
        kkf = kb * kk_ref[...]
        kb2 = kb * (1.0 + (a - 1.0) * ka_ref[...])
        bonus_src = r * kb2 * rk_ref[...]
        cums = _dot_exact_lhs(sel, logw)
        cum, tot = cums[:tb], cums[tb:]
        pin = jnp.exp(cum)
        pinv = jnp.exp(-cum)
        prem = jnp.exp(tot - cum)
        bonus = []
        for p in range(npair):
            sl = slice(p * LANES, (p + 1) * LANES)
            kf = kkf[:, sl]
            kn = kf * lax.rsqrt(jnp.maximum(_pair_sum(kf * kf, lo_full), 1e-24))
            bv = kn * a[:, sl]
            a_s[rs, sl] = -kn * jnp.exp(cum[:, sl] - logw[:, sl])
            b_s[rs, sl] = bv * pinv[:, sl]
            bp_s[rs, sl] = bv * prem[:, sl]
            bonus.append(_pair_sum(bonus_src[:, sl], lo_full) * v[:, sl])
        k_s[rs, :] = kb2 * pinv
        kp_s[rs, :] = kb2 * prem
        r_s[rs, :] = r * pin
        v_s[rs, :] = v
        pc_s[rs, :] = jnp.exp(tot)
        return bonus, lr[:, 2 * RWKV_W:]

    prepared = [prepare(s) for s in range(nseq)]

    c2 = 2 * chunk
    lo = _lo_mask((chunk, LANES))
    r2 = lax.broadcasted_iota(jnp.int32, (c2, c2), 0)
    q2 = lax.broadcasted_iota(jnp.int32, (c2, c2), 1)
    same = (r2 // chunk) == (q2 // chunk)
    strict = jnp.logical_and(same, (r2 % chunk) > (q2 % chunk))
    incl = jnp.logical_and(same, (r2 % chunk) >= (q2 % chunk))
    eye = (r2 == q2).astype(F32)
    n_sq = int(math.log2(chunk)) - 1
    pairs = range(nchain)
    lanes = [slice((c % npair) * LANES, (c % npair + 1) * LANES) for c in pairs]
    row0 = [(c // npair) * tb for c in pairs]

    def stack(x):
        return jnp.concatenate([jnp.where(lo, x, 0.0), jnp.where(lo, 0.0, x)], axis=0)

    def chunk_rows(ci):
        return [pl.ds(pl.multiple_of(row0[c] + ci * chunk, chunk), chunk) for c in pairs]

    def local_step(ci, carry):
        rows = chunk_rows(ci)
        ar = [jnp.concatenate([stack(a_s[rows[p], lanes[p]]), stack(r_s[rows[p], lanes[p]])], axis=0)
              for p in pairs]
        b2 = [b_s[rows[p], lanes[p]] for p in pairs]
        k2 = [k_s[rows[p], lanes[p]] for p in pairs]
        gb_ = [_dot_nt(ar[p], jnp.concatenate([b2[p], b2[p]], axis=0), hp=RWKV_HP_LOCAL) for p in pairs]
        gk_ = [_dot_nt(ar[p], jnp.concatenate([k2[p], k2[p]], axis=0), hp=RWKV_HP_LOCAL) for p in pairs]
        l_ab = [jnp.where(strict, gb_[p][:c2], 0.0) for p in pairs]
        tinv = [eye + l_ab[p] for p in pairs]
        xp = [_dot_x3(l_ab[p], l_ab[p]) for p in pairs]
        for _ in range(n_sq - 1):
            both = [_dot_x3(jnp.concatenate([tinv[p], xp[p]], axis=0), xp[p]) for p in pairs]
            tinv = [tinv[p] + both[p][:c2] for p in pairs]
            xp = [both[p][c2:] for p in pairs]
        tinv = [tinv[p] + _dot_x3(tinv[p], xp[p]) for p in pairs]
        for p in pairs:
            l_akrk = jnp.concatenate([jnp.where(strict, gk_[p][:c2], 0.0), jnp.where(incl, gk_[p][c2:], 0.0)], axis=0)
            lv = _dot(l_akrk, stack(v_s[rows[p], lanes[p]]), hp=RWKV_HP_LOCAL)
            tinv_s[ci, p] = tinv[p]
            lrb_s[ci, p] = jnp.where(incl, gb_[p][c2:], 0.0)
            lva_s[ci, p] = lv[:c2]
            lvr_s[ci, p] = lv[c2:]
        return carry

    lax.fori_loop(0, tb // chunk, local_step, 0)

    def state_step(ci, carry):
        rows = chunk_rows(ci)
        st = [st_refs[p][...] for p in pairs]
        ar = [jnp.concatenate([stack(a_s[rows[p], lanes[p]]), stack(r_s[rows[p], lanes[p]])], axis=0) for p in pairs]
        ars = [_dot_nt(ar[p], st[p], hp=RWKV_HP_STATE) for p in pairs]
        u_st = [_dot(tinv_s[ci, p], ars[p][:c2] + lva_s[ci, p], hp=RWKV_HP_STATE) for p in pairs]
        y_st = [ars[p][c2:] + lvr_s[ci, p] + _dot(lrb_s[ci, p], u_st[p], hp=RWKV_HP_STATE) for p in pairs]
        new = []
        for p in pairs:
            r, s = rows[p], lanes[p]
            uv = jnp.concatenate([u_st[p], stack(v_s[r, s])], axis=0)
            bk = jnp.concatenate([stack(bp_s[r, s]), stack(kp_s[r, s])], axis=0)
            pc = pc_s[pl.ds(row0[p] + ci * chunk, 1), s]
            new.append(st[p] * pc + _dot_tn(uv, bk, hp=RWKV_HP_STATE))
        for p in pairs:
            y_s[rows[p], lanes[p]] = y_st[p][:chunk] + y_st[p][chunk:]
            st_refs[p][...] = new[p]
        return carry

    lax.fori_loop(0, tb // chunk, state_step, 0)

    for s in range(nseq):
        bonus, gate = prepared[s]
        for p in range(npair):
            sl = slice(p * LANES, (p + 1) * LANES)
            y = y_s[s * tb:(s + 1) * tb, sl]
            mu = _pair_sum(y, lo_full) * (1.0 / RWKV_N)
            d = y - mu
            var = _pair_sum(d * d, lo_full) * (1.0 / RWKV_N)
            yn = (d * lax.rsqrt(var + 64e-5)) * lnw_ref[:, sl] + lnb_ref[:, sl]
            o_ref[s, :, sl] = (yn + bonus[p]) * gate[:, sl]

    @pl.when(blk == pl.num_programs(1) - 1)
    def _():
        for s in range(nseq):
            for p in range(npair):
                s_ref[s, p] = st_refs[s * npair + p][...]


def _rwkv(pb, prev, s0_bd, mu, wlr, w0, a0, k_k, k_a, r_k, ln_w, ln_b, tb, chunk):
    bsz, seq, _ = pb.shape
    npair = RWKV_HEADS // 2
    vec = lambda n: pl.BlockSpec((1, n), lambda b, i: (0, 0))
    nseq = RWKV_NSEQ
    nchain = nseq * npair
    scr = lambda: pltpu.VMEM((nseq * tb, RWKV_W), F32)
    nch, c2 = tb // chunk, 2 * chunk
    return pl.pallas_call(
        functools.partial(_rwkv_kernel, nseq, tb, chunk),
        grid=(bsz // nseq, seq // tb),
        in_specs=[
            pl.BlockSpec((nseq, tb, B_COLS), lambda b, i: (b, i, 0)),
            pl.BlockSpec((nseq, 1, B_COLS), lambda b, i: (b, 0, 0)),
            pl.BlockSpec((nseq, npair, LANES, LANES), lambda b, i: (b, 0, 0, 0)),
            vec(B_COLS),
            pl.BlockSpec(wlr.shape, lambda b, i: (0, 0)),
            vec(RWKV_W), vec(RWKV_W), vec(RWKV_W), vec(RWKV_W), vec(RWKV_W), vec(RWKV_W), vec(RWKV_W),
        ],
        out_specs=[
            pl.BlockSpec((nseq, tb, RWKV_W), lambda b, i: (b, i, 0)),
            pl.BlockSpec((nseq, npair, LANES, LANES), lambda b, i: (b, 0, 0, 0)),
        ],
        out_shape=[
            jax.ShapeDtypeStruct((bsz, seq, RWKV_W), F32),
            jax.ShapeDtypeStruct((bsz, npair, LANES, LANES), F32),
        ],
        scratch_shapes=[pltpu.VMEM((LANES, LANES), F32) for _ in range(nchain)] + [
            pltpu.VMEM((nseq, 1, B_COLS), F32),
            scr(), scr(), scr(), scr(), scr(), scr(), scr(), scr(), scr(),
            pltpu.VMEM((nch, nchain, c2, c2), F32), pltpu.VMEM((nch, nchain, c2, c2), F32),
            pltpu.VMEM((nch, nchain, c2, LANES), F32), pltpu.VMEM((nch, nchain, c2, LANES), F32),
        ],
        compiler_params=_cparams(("parallel", "arbitrary")),
        name="rwkv7",
    )(pb, prev, s0_bd, mu.reshape(1, -1), wlr, w0.reshape(1, -1), a0.reshape(1, -1), k_k.reshape(1, -1),
      k_a.reshape(1, -1), r_k.reshape(1, -1), ln_w.reshape(1, -1), ln_b.reshape(1, -1))


def _sb_block(z, mask, carry, tri_u):
    log_w, rem_sum = _sb_block_logs(z, mask, tri_u)
    return _sb_block_weights(log_w, carry, mask), carry + rem_sum


def _sb_block_logs(z, mask, tri_u):
    log_beta = jnp.minimum(z, 0.0) - jnp.log(1.0 + jnp.exp(-jnp.abs(z)))
    log_rem = log_beta - z
    if mask is not None:
        log_rem = jnp.where(mask, log_rem, 0.0)
    hi = log_rem.astype(BF16)
    lo = (log_rem - hi.astype(F32)).astype(BF16)
    after = jnp.dot(jnp.concatenate([hi, lo], axis=1), tri_u, preferred_element_type=F32)
    return log_beta + after, jnp.sum(log_rem, axis=-1, keepdims=True)


def _sb_block_weights(log_w, carry, mask):
    w = jnp.exp(log_w + carry)
    return w if mask is None else jnp.where(mask, w, 0.0)


def _strict_upper(n):
    r = lax.broadcasted_iota(jnp.int32, (2 * n, n), 0) % n
    c = lax.broadcasted_iota(jnp.int32, (2 * n, n), 1)
    return (r > c).astype(BF16)


def _sb_prompt_kernel(tq, tk, q_ref, k_ref, v_ref, o_ref):
    qi = pl.program_id(2)
    q = q_ref[0] * (SB_DH ** -0.5)
    lo = _lo_mask((tq, LANES))
    qa = jnp.where(lo, q, 0.0).astype(BF16)
    qb = jnp.where(lo, 0.0, q).astype(BF16)
    tri_u = _strict_upper(tk)
    qpos = qi * tq + lax.broadcasted_iota(jnp.int32, (tq, tk), 0)
    kofs = lax.broadcasted_iota(jnp.int32, (tq, tk), 1)
    n_diag = tq // tk
    n_full = qi * n_diag

    def visit(kb_idx, state, masked):
        acc_a, acc_b, car_a, car_b = state
        rows = pl.ds(pl.multiple_of(kb_idx * tk, tk), tk)
        kblk = k_ref[0, rows, :].astype(BF16)
        vblk = v_ref[0, rows, :].astype(BF16)
        mask = ((kb_idx * tk + kofs) < qpos) if masked else None
        w_a, car_a = _sb_block(_dot_nt(qa, kblk), mask, car_a, tri_u)
        w_b, car_b = _sb_block(_dot_nt(qb, kblk), mask, car_b, tri_u)
        acc_a = acc_a + jnp.dot(w_a.astype(BF16), vblk, preferred_element_type=F32)
        acc_b = acc_b + jnp.dot(w_b.astype(BF16), vblk, preferred_element_type=F32)
        return acc_a, acc_b, car_a, car_b

    zero = jnp.zeros((tq, LANES), F32)
    zc = jnp.zeros((tq, 1), F32)
    state = (zero, zero, zc, zc)
    for d in reversed(range(n_diag)):
        state = visit(n_full + d, state, True)
    state = lax.fori_loop(0, n_full, lambda j, s: visit(n_full - 1 - j, s, False), state)
    o_ref[0] = jnp.where(lo, state[0], state[1])


def _sb_prompt(q, k, v, tq, tk):
    bsz, seq, _ = q.shape
    npair = SB_HEADS // 2
    return pl.pallas_call(
        functools.partial(_sb_prompt_kernel, tq, tk),
        grid=(bsz, npair, seq // tq),
        in_specs=[
            pl.BlockSpec((1, tq, LANES), lambda b, h, i: (b, i, h)),
            pl.BlockSpec((1, seq, LANES), lambda b, h, i: (b, 0, h)),
            pl.BlockSpec((1, seq, LANES), lambda b, h, i: (b, 0, h)),
        ],
        out_specs=pl.BlockSpec((1, tq, LANES), lambda b, h, i: (b, i, h)),
        out_shape=jax.ShapeDtypeStruct((bsz, seq, SB_W), F32),
        compiler_params=_cparams(("parallel", "parallel", "arbitrary")),
        name="sb_prompt",
    )(q, k, v)


def _sb_sample_kernel(npg, lq, *refs):
    pt_ref = refs[0]
    q_ref, kn_ref, vn_ref = refs[1:4]
    kp_refs = refs[4:4 + npg]
    vp_refs = refs[4 + npg:4 + 2 * npg]
    o_ref = refs[4 + 2 * npg]
    qbd_ref, acc_ref, car_ref = refs[5 + 2 * npg:]
    del pt_ref
    j = pl.program_id(1)
    rows = SB_HEADS * lq
    tri_u = _strict_upper(PAGE)

    def visit_new(kblk, vblk, mask):
        w, car = _sb_block(_dot_nt(qbd_ref[...], kblk), mask, car_ref[...], tri_u)
        car_ref[...] = car
        acc_ref[...] += _dot(w, vblk)

    @pl.when(j == 0)
    def _():
        q = q_ref[0] * (SB_DH ** -0.5)
        qt = jnp.concatenate([q] * SB_HEADS, axis=0)
        rh = lax.broadcasted_iota(jnp.int32, (rows, SB_W), 0) // lq
        ch = lax.broadcasted_iota(jnp.int32, (rows, SB_W), 1) // SB_DH
        qbd_ref[...] = jnp.where(rh == ch, qt, 0.0).astype(BF16)
        acc_ref[...] = jnp.zeros_like(acc_ref)
        car_ref[...] = jnp.zeros_like(car_ref)
        pad = jnp.zeros((PAGE - lq, SB_W), F32)
        kblk = jnp.concatenate([kn_ref[0], pad], axis=0)
        vblk = jnp.concatenate([vn_ref[0], pad], axis=0)
        qidx = lax.broadcasted_iota(jnp.int32, (rows, PAGE), 0) % lq
        kidx = lax.broadcasted_iota(jnp.int32, (rows, PAGE), 1)
        visit_new(kblk, vblk, kidx < qidx)

    k_all = jnp.concatenate([kp_ref[0, 0].astype(BF16) for kp_ref in kp_refs], axis=1)
    z = jnp.dot(qbd_ref[...], k_all, preferred_element_type=F32)
    log_beta = jnp.minimum(z, 0.0) - jnp.log(1.0 + jnp.exp(-jnp.abs(z)))
    log_rem = log_beta - z
    hi = log_rem.astype(BF16)
    lo = (log_rem - hi.astype(F32)).astype(BF16)
    page = lambda t, p: t[:, p * PAGE:(p + 1) * PAGE]
    split = jnp.concatenate([jnp.concatenate([page(hi, p), page(lo, p)], axis=1) for p in range(npg)], axis=0)
    after = jnp.dot(split, tri_u, preferred_element_type=F32)
    car = car_ref[...]
    ws = []
    for p in range(npg):
        ws.append(jnp.exp(page(log_beta, p) + after[p * rows:(p + 1) * rows] + car).astype(BF16))
        car = car + jnp.sum(page(log_rem, p), axis=-1, keepdims=True)
    car_ref[...] = car
    v_all = jnp.concatenate([vp_ref[0, 0].astype(BF16) for vp_ref in vp_refs], axis=1)
    acc_ref[...] += _dot_nt(jnp.concatenate(ws, axis=1), v_all)

    @pl.when(j == pl.num_programs(1) - 1)
    def _():
        acc = acc_ref[...]
        ch = lax.broadcasted_iota(jnp.int32, (lq, SB_W), 1) // SB_DH
        out = jnp.zeros((lq, SB_W), F32)
        for h in range(SB_HEADS):
            out = out + jnp.where(ch == h, acc[h * lq:(h + 1) * lq, :], 0.0)
        o_ref[0] = out


def _sb_sample(q, k_new, v_new, cache_k, cache_v, page_table, npg):
    bsz, lq, _ = q.shape
    n_pages = page_table.shape[1]
    rows = SB_HEADS * lq
    tok = pl.BlockSpec((1, lq, SB_W), lambda b, j, pt: (b, 0, 0))

    def page_spec(i):
        return pl.BlockSpec((1, 1, SB_W, PAGE), lambda b, j, pt: (0, pt[b, n_pages - 1 - (j * npg + i)], 0, 0))

    grid_spec = pltpu.PrefetchScalarGridSpec(
        num_scalar_prefetch=1,
        grid=(bsz, n_pages // npg),
        in_specs=[tok, tok, tok] + [page_spec(i) for i in range(npg)] * 2,
        out_specs=pl.BlockSpec((1, lq, SB_W), lambda b, j, pt: (b, 0, 0)),
        scratch_shapes=[
            pltpu.VMEM((rows, SB_W), BF16),
            pltpu.VMEM((rows, SB_W), F32),
            pltpu.VMEM((rows, 1), F32),
        ],
    )
    return pl.pallas_call(
        functools.partial(_sb_sample_kernel, npg, lq),
        grid_spec=grid_spec,
        out_shape=jax.ShapeDtypeStruct((bsz, lq, SB_W), F32),
        compiler_params=_cparams(("parallel", "arbitrary")),
        name="sb_sample",
    )(page_table, q, k_new, v_new, *([cache_k] * npg), *([cache_v] * npg))


def _first_index(cond, idx, big):
    return jnp.min(jnp.where(cond, idx, big), axis=(0, 1), keepdims=True)


def _router_kernel(x_ref, g_ref, sh_ref, sc_ref, rw_ref, rb_ref, h_ref, comb_ref, cnt_ref):
    h = _modulated_norm(x_ref[0], g_ref[...], sh_ref[0], sc_ref[0])
    h_ref[0] = h.astype(BF16)
    tl = h.shape[0]
    logits = _dot_nt(rw_ref[...], h, hp=True)
    scores = jax.nn.sigmoid(logits).reshape(N_GROUPS, GROUP_SIZE, tl)
    biased = scores + rb_ref[...].reshape(N_GROUPS, GROUP_SIZE, 1)
    neg = -jnp.inf

    jidx = lax.broadcasted_iota(jnp.int32, biased.shape, 1)
    m1 = jnp.max(biased, axis=1, keepdims=True)
    first = jnp.min(jnp.where(biased == m1, jidx, GROUP_SIZE), axis=1, keepdims=True)
    m2 = jnp.max(jnp.where(jidx == first, neg, biased), axis=1, keepdims=True)
    gscore = m1 + m2

    gidx = lax.broadcasted_iota(jnp.int32, gscore.shape, 0)
    gsel = jnp.zeros(gscore.shape, jnp.bool_)
    for _ in range(TOPK_GROUPS):
        m = jnp.max(gscore, axis=0, keepdims=True)
        pick = gidx == jnp.min(jnp.where(gscore == m, gidx, N_GROUPS), axis=0, keepdims=True)
        gsel = jnp.logical_or(gsel, pick)
        gscore = jnp.where(pick, neg, gscore)

    cand = jnp.where(gsel, biased, neg)
    eidx = lax.broadcasted_iota(jnp.int32, cand.shape, 0) * GROUP_SIZE + jidx
    esel = jnp.zeros(cand.shape, jnp.bool_)
    for _ in range(TOP_K):
        m = jnp.max(cand, axis=(0, 1), keepdims=True)
        pick = eidx == _first_index(cand == m, eidx, N_EXPERTS)
        esel = jnp.logical_or(esel, pick)
        cand = jnp.where(pick, neg, cand)

    sel = jnp.where(esel, scores, 0.0)
    wts = (sel / jnp.sum(sel, axis=(0, 1), keepdims=True) * ROUTED_SCALE).reshape(N_EXPERTS, tl)
    comb_ref[0] = wts
    cnt = jnp.sum((wts > 0.0).astype(jnp.int32), axis=-1, keepdims=True)
    cnt_ref[0] = jnp.broadcast_to(cnt, (N_EXPERTS, LANES))


def _router(x3, g, sh3, sc3, router_w, router_bias):
    gsz, lg, _ = x3.shape
    tl = MOE_TB
    tpg = lg // tl
    lm = sh3.shape[1]
    tm = 1 if lm == 1 else tl
    mod_map = (lambda b, i: (b, 0, 0)) if lm == 1 else (lambda b, i: (b, i, 0))
    return pl.pallas_call(
        _router_kernel,
        grid=(gsz, tpg),
        in_specs=[
            pl.BlockSpec((1, tl, D_MODEL), lambda b, i: (b, i, 0)),
            pl.BlockSpec((1, D_MODEL), lambda b, i: (0, 0)),
            pl.BlockSpec((1, tm, D_MODEL), mod_map),
            pl.BlockSpec((1, tm, D_MODEL), mod_map),
            pl.BlockSpec((N_EXPERTS, D_MODEL), lambda b, i: (0, 0)),
            pl.BlockSpec((N_EXPERTS, 1), lambda b, i: (0, 0)),
        ],
        out_specs=[
            pl.BlockSpec((1, tl, D_MODEL), lambda b, i: (b, i, 0)),
            pl.BlockSpec((1, N_EXPERTS, tl), lambda b, i: (b * tpg + i, 0, 0)),
            pl.BlockSpec((1, N_EXPERTS, LANES), lambda b, i: (b * tpg + i, 0, 0)),
        ],
        out_shape=[
            jax.ShapeDtypeStruct((gsz, lg, D_MODEL), BF16),
            jax.ShapeDtypeStruct((gsz * tpg, N_EXPERTS, tl), F32),
            jax.ShapeDtypeStruct((gsz * tpg, N_EXPERTS, LANES), jnp.int32),
        ],
        compiler_params=_cparams(("parallel", "parallel")),
        name="moe_router",
    )(x3, g.reshape(1, D_MODEL), sh3, sc3, router_w.T, router_bias.reshape(N_EXPERTS, 1))


def _swiglu_hidden(h, wg, wu):
    return _silu(jnp.dot(h, wg, preferred_element_type=F32)) * jnp.dot(h, wu, preferred_element_type=F32)


def _moe_plan(cnt, nt_max):
    e_ids = N_EXPERTS
    pc = (cnt + MOE_UNIT - 1) // MOE_UNIT * MOE_UNIT
    lend = jnp.cumsum(pc, axis=1)
    lstart = lend - pc
    nun = (lend[:, -1] // MOE_UNIT).astype(jnp.int32)
    ctot = jnp.sum(pc, axis=0)
    rs = (ctot + MOE_RT - 1) // MOE_RT * MOE_RT
    gend = jnp.cumsum(rs)
    gstart = gend - rs
    toff = jnp.cumsum(pc, axis=0) - pc
    urow = jnp.arange(MOE_NU, dtype=jnp.int32) * MOE_UNIT
    eu = jnp.sum((urow[None, :, None] >= lend[:, None, :]).astype(jnp.int32), axis=-1)
    valid = eu < e_ids
    run_start = jnp.max(jnp.where(lstart[:, None, :] <= urow[None, :, None], lstart[:, None, :], 0), axis=-1)
    uinfo = jnp.where(valid, eu * MOE_INFO_SHIFT + (urow[None, :] - run_start), e_ids * MOE_INFO_SHIFT)
    gbase = (gstart[None, :] + toff) // MOE_UNIT
    zn = ((rs - ctot) // MOE_UNIT).astype(jnp.int32)
    zdst = (gstart + ctot)[:, None] // MOE_UNIT + jnp.arange(MOE_RT // MOE_UNIT, dtype=jnp.int32)[None, :]
    n_used = (gend[-1] // MOE_RT).astype(jnp.int32).reshape(1)
    tmap = jnp.sum((jnp.arange(nt_max, dtype=jnp.int32)[:, None] * MOE_RT >= gend[None, :]).astype(jnp.int32), axis=-1)
    tmap = jnp.minimum(tmap, e_ids - 1)
    i32 = lambda a: a.astype(jnp.int32).reshape(-1)
    return dict(uinfo=i32(uinfo), gbase=i32(gbase),nun=nun, zn=zn, zdst=i32(zdst), tmap=i32(tmap), n_used=n_used)


def _routing_rank(sel, rank_s, wsel_s=None):
    tb = sel.shape[1]
    mask = sel > 0.0
    before = (lax.broadcasted_iota(jnp.int32, (tb, tb), 0) < lax.broadcasted_iota(jnp.int32, (tb, tb), 1))
    rank = jnp.dot(mask.astype(BF16), before.astype(BF16), preferred_element_type=F32)
    rank_s[0:N_EXPERTS, :] = jnp.where(mask, rank, -1.0)
    rank_s[N_EXPERTS:, :] = jnp.full((8, tb), -1.0, F32)
    if wsel_s is not None:
        wsel_s[0:N_EXPERTS, :] = sel
        wsel_s[N_EXPERTS:, :] = jnp.zeros((8, tb), F32)


def _unit_rows(uinfo_ref, gbase_ref, i, u, rank_s, gunit_s):
    info = uinfo_ref[i * MOE_NU + u]
    e = lax.shift_right_logical(info, MOE_INFO_BITS)
    off = jnp.bitwise_and(info, MOE_INFO_SHIFT - 1)
    gunit_s[u] = (gbase_ref[i * N_EXPERTS + jnp.minimum(e, N_EXPERTS - 1)]
                  + lax.shift_right_logical(off, MOE_UNIT_BITS))
    j = lax.broadcasted_iota(jnp.int32, (MOE_UNIT, MOE_TB), 0).astype(F32)
    return rank_s[pl.ds(e, 1), :] == (j + off.astype(F32)), e


def _for_each_unit(n, fn):
    def group(g, c):
        for k in range(MOE_ISSUE):
            fn(g * MOE_ISSUE + k)
        return c

    def single(u, c):
        fn(u)
        return c

    full = lax.shift_right_logical(n, MOE_ISSUE_BITS)
    lax.fori_loop(0, full, group, 0)
    lax.fori_loop(full * MOE_ISSUE, n, single, 0)


def _wait_units(n, src, dst, sem):
    @pl.when(n > 0)
    def _():
        rows = pl.ds(0, n * MOE_UNIT)
        pltpu.make_async_copy(src.at[rows], dst.at[rows], sem).wait()


def _unit_copy(src, dst, sem):
    return pltpu.make_async_copy(src, dst, sem)


def _moe_dispatch_kernel(uinfo_ref, gbase_ref, nun_ref, zn_ref, zdst_ref, h_ref, sel_ref, xs_hbm,
                         p_s, xs_s, rank_s, zero_s, gunit_s, sem, zsem):
    i = pl.program_id(0)
    n = nun_ref[i]
    unit = lambda ref, u: ref.at[pl.ds(pl.multiple_of(u * MOE_UNIT, MOE_UNIT), MOE_UNIT)]

    @pl.when(i == 0)
    def _():
        zero_s[...] = jnp.zeros_like(zero_s)
        zper = MOE_RT // MOE_UNIT

        def per_expert(e, tot):
            def one(z, c):
                _unit_copy(zero_s, unit(xs_hbm, zdst_ref[e * zper + z]), zsem).start()
                return c
            lax.fori_loop(0, zn_ref[e], one, 0)
            return tot + zn_ref[e]

        total = lax.fori_loop(0, N_EXPERTS, per_expert, 0)

        def wait_one(z, c):
            _unit_copy(zero_s, unit(xs_hbm, 0), zsem).wait()
            return c
        lax.fori_loop(0, total, wait_one, 0)

    _routing_rank(sel_ref[0], rank_s)
    for u in range(MOE_NU):
        hit, _ = _unit_rows(uinfo_ref, gbase_ref, i, u, rank_s, gunit_s)
        p_s[u * MOE_UNIT:(u + 1) * MOE_UNIT, :] = hit.astype(BF16)

    h = h_ref[0]
    for c in range(MOE_RMAX // MOE_CH):
        @pl.when(c * (MOE_CH // MOE_UNIT) < n)
        def _():
            rows = slice(c * MOE_CH, (c + 1) * MOE_CH)
            xs_s[rows, :] = jnp.dot(p_s[rows, :], h, preferred_element_type=F32).astype(BF16)

    _for_each_unit(n, lambda u: _unit_copy(unit(xs_s, u), unit(xs_hbm, gunit_s[u]), sem).start())
    _wait_units(n, xs_s, xs_hbm, sem)


def _moe_dispatch(h3, sel, plan, pmax):
    gsz, lg, _ = h3.shape
    tpg = lg // MOE_TB
    n_tiles = gsz * tpg
    grid_spec = pltpu.PrefetchScalarGridSpec(
        num_scalar_prefetch=5,
        grid=(n_tiles,),
        in_specs=[
            pl.BlockSpec((1, MOE_TB, D_MODEL), lambda i, *_: (i // tpg, i % tpg, 0)),
            pl.BlockSpec((1, N_EXPERTS, MOE_TB), lambda i, *_: (i, 0, 0)),
        ],
        out_specs=pl.BlockSpec(memory_space=pl.ANY),
        scratch_shapes=[
            pltpu.VMEM((MOE_RMAX, MOE_TB), BF16),
            pltpu.VMEM((MOE_RMAX, D_MODEL), BF16),
            pltpu.VMEM((N_EXPERTS + 8, MOE_TB), F32),
            pltpu.VMEM((MOE_UNIT, D_MODEL), BF16),
            pltpu.SMEM((MOE_NU,), jnp.int32),
            pltpu.SemaphoreType.DMA(()),
            pltpu.SemaphoreType.DMA(()),
        ],
    )
    return pl.pallas_call(
        _moe_dispatch_kernel,
        grid_spec=grid_spec,
        out_shape=jax.ShapeDtypeStruct((pmax, D_MODEL), BF16),
        compiler_params=_cparams(("arbitrary",)),
        name="moe_dispatch",
    )(plan['uinfo'], plan['gbase'], plan['nun'], plan['zn'], plan['zdst'], h3, sel)


def _moe_group_kernel(tmap_ref, nused_ref, xs_ref, wg_ref, wu_ref, wd_ref, ys_ref, wg_s, wu_s, wd_s):
    j = pl.program_id(0)

    @pl.when(j < nused_ref[0])
    def _():
        @pl.when(jnp.logical_or(j == 0, tmap_ref[j] != tmap_ref[jnp.maximum(j - 1, 0)]))
        def _():
            wg_s[...] = wg_ref[0, 0].astype(BF16)
            wu_s[...] = wu_ref[0, 0].astype(BF16)
            wd_s[...] = wd_ref[0, 0].astype(BF16)

        act = _swiglu_hidden(xs_ref[...], wg_s[...], wu_s[...])
        ys_ref[...] = jnp.dot(act.astype(BF16), wd_s[...], preferred_element_type=F32).astype(BF16)


def _moe_group(xs, plan, wg, wu, wd, layer):
    pmax = xs.shape[0]
    nt = pmax // MOE_RT
    row_map = lambda j, tmap, nused: (jnp.maximum(jnp.minimum(j, nused[0] - 1), 0), 0)
    wmap = lambda j, tmap, nused: (layer, tmap[j], 0, 0)
    grid_spec = pltpu.PrefetchScalarGridSpec(
        num_scalar_prefetch=2,
        grid=(nt,),
        in_specs=[
            pl.BlockSpec((MOE_RT, D_MODEL), row_map),
            pl.BlockSpec((1, 1, D_MODEL, HIDDEN), wmap),
            pl.BlockSpec((1, 1, D_MODEL, HIDDEN), wmap),
            pl.BlockSpec((1, 1, HIDDEN, D_MODEL), wmap),
        ],
        out_specs=pl.BlockSpec((MOE_RT, D_MODEL), row_map),
        scratch_shapes=[
            pltpu.VMEM((D_MODEL, HIDDEN), BF16),
            pltpu.VMEM((D_MODEL, HIDDEN), BF16),
            pltpu.VMEM((HIDDEN, D_MODEL), BF16),
        ],
    )
    return pl.pallas_call(
        _moe_group_kernel,
        grid_spec=grid_spec,
        out_shape=jax.ShapeDtypeStruct((pmax, D_MODEL), BF16),
        compiler_params=_cparams(("arbitrary",)),
        name="moe_group",
    )(plan['tmap'], plan['n_used'], xs, wg, wu, wd)


def _moe_combine_kernel(final, uinfo_ref, gbase_ref, nun_ref, h_ref, sel_ref, x_ref, gate_ref, fn_ref,
                        sg_ref, su_ref, sd_ref, ys_hbm, o_ref, pw_s, ys_s, rank_s, wsel_s, acc_s, gunit_s, sem):
    i = pl.program_id(0)
    n = nun_ref[i]
    unit = lambda ref, u: ref.at[pl.ds(pl.multiple_of(u * MOE_UNIT, MOE_UNIT), MOE_UNIT)]

    @pl.when(i == 0)
    def _():
        ys_s[...] = jnp.zeros_like(ys_s)

    _routing_rank(sel_ref[0], rank_s, wsel_s)
    for u in range(MOE_NU):
        hit, e = _unit_rows(uinfo_ref, gbase_ref, i, u, rank_s, gunit_s)
        pw_s[u * MOE_UNIT:(u + 1) * MOE_UNIT, :] = jnp.where(hit, wsel_s[pl.ds(e, 1), :], 0.0).astype(BF16)

    _for_each_unit(n, lambda u: _unit_copy(unit(ys_hbm, gunit_s[u]), unit(ys_s, u), sem).start())

    h = h_ref[0]
    act = _swiglu_hidden(h, sg_ref[...], su_ref[...])
    acc_s[...] = jnp.dot(act.astype(BF16), sd_ref[...], preferred_element_type=F32)

    _wait_units(n, ys_hbm, ys_s, sem)

    for c in range(MOE_RMAX // MOE_CH):
        @pl.when(c * (MOE_CH // MOE_UNIT) < n)
        def _():
            rows = slice(c * MOE_CH, (c + 1) * MOE_CH)
            acc_s[...] += _dot_tn(pw_s[rows, :], ys_s[rows, :])

    y = x_ref[0] + gate_ref[0] * acc_s[...]
    if final:
        y = (y * lax.rsqrt(jnp.mean(y * y, axis=-1, keepdims=True) + EPS)) * fn_ref[...]
    o_ref[0] = y


def _moe_combine(x3, gate3, h3, sel, ys, plan, sg, su, sd, final_norm, final):
    gsz, lg, _ = x3.shape
    tpg = lg // MOE_TB
    lm = gate3.shape[1]
    tm = 1 if lm == 1 else MOE_TB
    tok_map = lambda i, *_: (i // tpg, i % tpg, 0)
    mod_map = (lambda i, *_: (i // tpg, 0, 0)) if lm == 1 else tok_map
    full = lambda a: pl.BlockSpec(a.shape, lambda i, *_: (0, 0))
    grid_spec = pltpu.PrefetchScalarGridSpec(
        num_scalar_prefetch=3,
        grid=(gsz * tpg,),
        in_specs=[
            pl.BlockSpec((1, MOE_TB, D_MODEL), tok_map),
            pl.BlockSpec((1, N_EXPERTS, MOE_TB), lambda i, *_: (i, 0, 0)),
            pl.BlockSpec((1, MOE_TB, D_MODEL), tok_map),
            pl.BlockSpec((1, tm, D_MODEL), mod_map),
            pl.BlockSpec((1, D_MODEL), lambda i, *_: (0, 0)),
            full(sg), full(su), full(sd),
            pl.BlockSpec(memory_space=pl.ANY),
        ],
        out_specs=pl.BlockSpec((1, MOE_TB, D_MODEL), tok_map),
        scratch_shapes=[
            pltpu.VMEM((MOE_RMAX, MOE_TB), BF16),
            pltpu.VMEM((MOE_RMAX, D_MODEL), BF16),
            pltpu.VMEM((N_EXPERTS + 8, MOE_TB), F32),
            pltpu.VMEM((N_EXPERTS + 8, MOE_TB), F32),
            pltpu.VMEM((MOE_TB, D_MODEL), F32),
            pltpu.SMEM((MOE_NU,), jnp.int32),
            pltpu.SemaphoreType.DMA(()),
        ],
    )
    return pl.pallas_call(
        functools.partial(_moe_combine_kernel, final),
        grid_spec=grid_spec,
        out_shape=jax.ShapeDtypeStruct(x3.shape, F32),
        compiler_params=_cparams(("arbitrary",)),
        name="moe_combine",
    )(plan['uinfo'], plan['gbase'], plan['nun'], h3, sel, x3, gate3, final_norm.reshape(1, D_MODEL), sg, su, sd, ys)


def _block_diag_pairs(s):
    bsz, nh, n, _ = s.shape
    s = s.reshape(bsz, nh // 2, 2, n, n)
    z = jnp.zeros_like(s[:, :, 0])
    top = jnp.concatenate([s[:, :, 0], z], axis=-1)
    bot = jnp.concatenate([z, s[:, :, 1]], axis=-1)
    return jnp.concatenate([top, bot], axis=-2)


def _unpair(s_bd):
    bsz, npair, n2, _ = s_bd.shape
    n = n2 // 2
    return jnp.stack([s_bd[:, :, :n, :n], s_bd[:, :, n:, n:]], axis=2).reshape(bsz, npair * 2, n, n)


def _rotary_tables(pos):
    half = RET_DK // 2
    inv = ROPE_BASE ** (-jnp.arange(half, dtype=F32) / half)
    ang = pos.astype(F32)[:, None] * inv[None, :]
    cos, sin = jnp.cos(ang), jnp.sin(ang)
    cos_t = jnp.concatenate([cos, cos, cos, cos], axis=-1)
    sin_t = jnp.concatenate([-sin, sin, -sin, sin], axis=-1)
    return cos_t, sin_t


def _trunk(x, c, pos, ret_s, rwkv_s, shift_s, cache, p, flat):
    bsz, seq, _ = x.shape
    mod = _ada(c, p['ada_w'], p['ada_b'])
    mod = mod.reshape(DEPTH, bsz, 6, D_MODEL)
    if flat:
        tok = lambda t: t.reshape(1, bsz * seq, t.shape[-1])
        untok = lambda t: t.reshape(bsz, seq, t.shape[-1])
        modv = lambda l, j: jnp.broadcast_to(mod[l, :, j][:, None, :], (bsz, seq, D_MODEL)).reshape(1, bsz * seq, D_MODEL)
        tl_proj = bsz * seq
    else:
        tok = untok = lambda t: t
        modv = lambda l, j: mod[l, :, j][:, None, :]
        tl_proj = min(256, seq)

    x3 = tok(x)
    w_in = p['w_in_ab'][0].astype(BF16)
    a_cols = 4 * RET_W
    zeros_a = jnp.zeros((1, a_cols), F32)
    zeros_b = jnp.zeros((1, B_COLS), F32)
    pa, pb = _norm_proj(x3, p['norm_mix'][0], modv(0, 0), modv(0, 1), [w_in[:, :a_cols], w_in[:, a_cols:]],
                        [zeros_a, zeros_b], tl_proj)
    pa, pb = untok(pa), untok(pb)
    cos_t, sin_t = _rotary_tables(pos)
    lgs = jnp.log1p(-jnp.exp2(-5.0 - jnp.arange(RET_HEADS, dtype=F32)))
    ret_chunk = math.gcd(seq, 128)
    o_a, ret_new = _retention(pa, cos_t, sin_t, lgs, p['ret_gn_w'][0], p['ret_gn_b'][0],
                              _block_diag_pairs(ret_s), ret_chunk)
    wlr = jnp.zeros((LOWRANK, 3 * RWKV_W), F32)
    wlr = wlr.at[0:64, 0:RWKV_W].set(p['rwkv_w_up'][0])
    wlr = wlr.at[64:128, RWKV_W:2 * RWKV_W].set(p['rwkv_a_up'][0])
    wlr = wlr.at[128:256, 2 * RWKV_W:].set(p['rwkv_g_up'][0])
    tb = min(seq, 128)
    o_b, rwkv_new = _rwkv(pb, shift_s[:, None, :], _block_diag_pairs(rwkv_s), p['rwkv_mu'][0], wlr.astype(BF16),
                          p['rwkv_w0'][0], p['rwkv_a0'][0], p['rwkv_k_k'][0], p['rwkv_k_a'][0],
                          p['rwkv_r_k'][0].reshape(-1), p['rwkv_ln_w'][0], p['rwkv_ln_b'][0],
                          tb, min(RWKV_CHUNK, seq))
    w_out = p['w_out_ab'][0].astype(BF16)
    x3 = _out_proj(x3, modv(0, 2), [tok(o_a), tok(o_b)], [w_out[:RET_W], w_out[RET_W:]], tl_proj)
    x3 = _moe_layer(x3, 0, modv, p, final=False)

    wq, wk, wv = jnp.split(p['w_qkv_c'][0].astype(BF16), 3, axis=1)
    q, k, v = _norm_proj(x3, p['norm_mix'][1], modv(1, 0), modv(1, 1), [wq, wk, wv],
                         [p['b_q_c'][0][None, :], p['b_k_c'][0][None, :], jnp.zeros((1, SB_W), F32)], tl_proj)
    q, k, v = untok(q), untok(k), untok(v)
    if cache is None:
        o = _sb_prompt(q, k, v, min(512, seq), min(256, seq))
    else:
        cache_k, cache_v, page_table = cache
        o = _sb_sample(q, k, v, cache_k, cache_v, page_table, 8)
    x3 = _out_proj(x3, modv(1, 2), [tok(o)], [p['w_out_c'][0].astype(BF16)], tl_proj)
    y3 = _moe_layer(x3, 1, modv, p, final=True)

    kv_shape = (1, bsz, seq, SB_HEADS, SB_DH)
    return (untok(y3), _unpair(ret_new)[None], _unpair(rwkv_new)[None], pb[:, -1][None],
            k.reshape(kv_shape), v.reshape(kv_shape))


def _moe_layer(x3, l, modv, p, final):
    gsz, lg, _ = x3.shape
    n_tiles = gsz * lg // MOE_TB
    pmax = gsz * lg * TOP_K + n_tiles * N_EXPERTS * (MOE_UNIT - 1) + N_EXPERTS * (MOE_RT - MOE_UNIT)
    pmax = -(-pmax // MOE_RT) * MOE_RT
    h3, sel, cnt = _router(x3, p['norm_ffn'][l], modv(l, 3), modv(l, 4), p['router_w'][l], p['router_bias'][l])
    plan = _moe_plan(cnt[:, :, 0], pmax // MOE_RT)
    xs = _moe_dispatch(h3, sel, plan, pmax)
    ys = _moe_group(xs, plan, p['exp_w_gate'], p['exp_w_up'], p['exp_w_down'], l)
    return _moe_combine(x3, modv(l, 5), h3, sel, ys, plan, p['sh_w_gate'][l].astype(BF16),
                        p['sh_w_up'][l].astype(BF16), p['sh_w_down'][l].astype(BF16), p['final_norm'], final)


def kernel(x_prompt, x_sample, c_prompt, c_sample, state_ret, state_rwkv, state_shift, cache_k, cache_v, page_table, ada_w, ada_b, norm_mix, norm_ffn, final_norm, w_in_ab, w_out_ab, ret_gn_w, ret_gn_b, rwkv_mu, rwkv_w0, rwkv_w_up, rwkv_a0, rwkv_a_up, rwkv_g_up, rwkv_k_k, rwkv_k_a, rwkv_r_k, rwkv_ln_w, rwkv_ln_b, w_qkv_c, b_q_c, b_k_c, w_out_c, router_w, router_bias, exp_w_gate, exp_w_up, exp_w_down, sh_w_gate, sh_w_up, sh_w_down):
    p = dict(ada_w=ada_w, ada_b=ada_b, norm_mix=norm_mix, norm_ffn=norm_ffn, final_norm=final_norm,
             w_in_ab=w_in_ab, w_out_ab=w_out_ab, ret_gn_w=ret_gn_w, ret_gn_b=ret_gn_b, rwkv_mu=rwkv_mu,
             rwkv_w0=rwkv_w0, rwkv_w_up=rwkv_w_up, rwkv_a0=rwkv_a0, rwkv_a_up=rwkv_a_up, rwkv_g_up=rwkv_g_up,
             rwkv_k_k=rwkv_k_k, rwkv_k_a=rwkv_k_a, rwkv_r_k=rwkv_r_k, rwkv_ln_w=rwkv_ln_w, rwkv_ln_b=rwkv_ln_b,
             w_qkv_c=w_qkv_c, b_q_c=b_q_c, b_k_c=b_k_c, w_out_c=w_out_c, router_w=router_w,
             router_bias=router_bias, exp_w_gate=exp_w_gate, exp_w_up=exp_w_up, exp_w_down=exp_w_down,
             sh_w_gate=sh_w_gate, sh_w_up=sh_w_up, sh_w_down=sh_w_down)
    bp, lp, _ = x_prompt.shape
    bs, ls, _ = x_sample.shape
    zeros = lambda *s: jnp.zeros(s, F32)
    y_p, ret_p, rwkv_p, shift_p, k_p, v_p = _trunk(
        x_prompt, c_prompt, jnp.arange(lp), zeros(bp, RET_HEADS, RET_DK, RET_DK),
        zeros(bp, RWKV_HEADS, RWKV_N, RWKV_N), zeros(bp, B_COLS), None, p, flat=False)
    n_pages = page_table.shape[1]
    n_pool = cache_k.shape[1]
    pages = lambda t: jnp.transpose(t, (0, 1, 3, 4, 2)).reshape(1, n_pool, SB_W, PAGE)
    cache = (pages(cache_k), pages(cache_v), page_table)
    y_s, ret_s, rwkv_s, shift_s, k_s, v_s = _trunk(
        x_sample, c_sample, n_pages * PAGE + jnp.arange(ls), state_ret[0], state_rwkv[0], state_shift[0],
        cache, p, flat=True)
    return (y_p, y_s, ret_p, ret_s, rwkv_p, rwkv_s, shift_p, shift_s, k_p, v_p, k_s, v_s)
```

```python
import functools
import math

import jax
import jax.numpy as jnp
from jax import lax
from jax.experimental import pallas as pl
from jax.experimental.pallas import tpu as pltpu

F32 = jnp.float32
BF16 = jnp.bfloat16
HIGHEST = lax.Precision.HIGHEST

D_MODEL = 1024
DEPTH = 2
PAGE = 128
RET_HEADS = 8
RET_DK = 64
RWKV_HEADS = 8
RWKV_N = 64
RWKV_W = RWKV_HEADS * RWKV_N
RET_W = RET_HEADS * RET_DK
LOWRANK = 256
B_COLS = 3 * RWKV_W + LOWRANK
SB_HEADS = 16
SB_DH = 64
SB_W = SB_HEADS * SB_DH
SB_SCALE = SB_DH ** -0.5 * math.log2(math.e)
N_EXPERTS = 64
N_GROUPS = 8
GROUP_SIZE = N_EXPERTS // N_GROUPS
TOPK_GROUPS = 4
TOP_K = 8
HIDDEN = 256
ROUTED_SCALE = 2.5
EPS = 1e-6
ROPE_BASE = 10000.0
LANES = 128
VMEM_LIMIT = 56 * 1024 * 1024

MOE_TB = 256
MOE_UNIT_BITS = 4
MOE_UNIT = 1 << MOE_UNIT_BITS
MOE_RT = 512
MOE_CH = 512
MOE_RMAX = -(-(MOE_TB * TOP_K + N_EXPERTS * (MOE_UNIT - 1)) // MOE_CH) * MOE_CH
MOE_NU = MOE_RMAX // MOE_UNIT
MOE_ISSUE_BITS = 3
MOE_ISSUE = 1 << MOE_ISSUE_BITS
MOE_INFO_BITS = 9
MOE_INFO_SHIFT = 1 << MOE_INFO_BITS

RWKV_CHUNK = 16
RWKV_NSEQ = 4
RWKV_HP_LOCAL = False
RWKV_HP_STATE = False


def _cparams(sem):
    return pltpu.CompilerParams(dimension_semantics=sem, vmem_limit_bytes=VMEM_LIMIT)


def _dot(a, b, hp=False):
    if hp:
        return jnp.dot(a.astype(F32), b.astype(F32), preferred_element_type=F32, precision=HIGHEST)
    return jnp.dot(a.astype(BF16), b.astype(BF16), preferred_element_type=F32)


def _dot_nt(a, b, hp=False):
    dn = (((1,), (1,)), ((), ()))
    if hp:
        return lax.dot_general(a.astype(F32), b.astype(F32), dn, preferred_element_type=F32, precision=HIGHEST)
    return lax.dot_general(a.astype(BF16), b.astype(BF16), dn, preferred_element_type=F32)


def _dot_tn(a, b, hp=False):
    dn = (((0,), (0,)), ((), ()))
    if hp:
        return lax.dot_general(a.astype(F32), b.astype(F32), dn, preferred_element_type=F32, precision=HIGHEST)
    return lax.dot_general(a.astype(BF16), b.astype(BF16), dn, preferred_element_type=F32)


def _split_bf16(x):
    hi = x.astype(BF16)
    return hi, x - hi.astype(F32)


def _dot_x3(a, b):
    ah, al = _split_bf16(a)
    bh, bl = _split_bf16(b)
    d = lambda x, y: jnp.dot(x, y.astype(BF16), preferred_element_type=F32)
    return d(ah, bh) + (d(ah, bl) + d(al.astype(BF16), bh))


def _dot_exact_lhs(a, b):
    a = a.astype(BF16)
    b1, r1 = _split_bf16(b)
    b2, r2 = _split_bf16(r1)
    d = lambda y: jnp.dot(a, y.astype(BF16), preferred_element_type=F32)
    return d(b1) + (d(b2) + d(r2))


def _silu(x):
    return x * jax.nn.sigmoid(x)


def _lo_mask(shape):
    return (lax.broadcasted_iota(jnp.int32, shape, len(shape) - 1) % LANES) < (LANES // 2)


def _pair_sum(x, lo):
    s_lo = jnp.sum(jnp.where(lo, x, 0.0), axis=-1, keepdims=True)
    s_hi = jnp.sum(jnp.where(lo, 0.0, x), axis=-1, keepdims=True)
    return jnp.where(lo, s_lo, s_hi)


def _ada_kernel(c_ref, w_ref, b_ref, o_ref):
    o_ref[0] = _dot(_silu(c_ref[...]), w_ref[0]) + b_ref[0]


def _ada(c, ada_w, ada_b):
    bsz = c.shape[0]
    n = ada_w.shape[-1]
    tn = D_MODEL
    return pl.pallas_call(
        _ada_kernel,
        grid=(DEPTH, n // tn),
        in_specs=[
            pl.BlockSpec((bsz, D_MODEL), lambda l, j: (0, 0)),
            pl.BlockSpec((1, D_MODEL, tn), lambda l, j: (l, 0, j)),
            pl.BlockSpec((1, 1, tn), lambda l, j: (l, 0, j)),
        ],
        out_specs=pl.BlockSpec((1, bsz, tn), lambda l, j: (l, 0, j)),
        out_shape=jax.ShapeDtypeStruct((DEPTH, bsz, n), F32),
        compiler_params=_cparams(("parallel", "parallel")),
        name="ada_mod",
    )(c, ada_w, ada_b.reshape(DEPTH, 1, n))


def _modulated_norm(x, g, sh, sc):
    y = x * lax.rsqrt(jnp.mean(x * x, axis=-1, keepdims=True) + EPS)
    return (y * g) * (1.0 + sc) + sh


def _norm_proj_kernel(n_out, x_ref, g_ref, sh_ref, sc_ref, *refs):
    w_refs, b_refs, o_refs = refs[:n_out], refs[n_out:2 * n_out], refs[2 * n_out:]
    h = _modulated_norm(x_ref[0], g_ref[...], sh_ref[0], sc_ref[0]).astype(BF16)
    for w_ref, b_ref, o_ref in zip(w_refs, b_refs, o_refs):
        o_ref[0] = jnp.dot(h, w_ref[...], preferred_element_type=F32) + b_ref[...]


def _norm_proj(x3, g, sh3, sc3, ws, bs, tl):
    gsz, lg, _ = x3.shape
    lm = sh3.shape[1]
    tm = 1 if lm == 1 else tl
    mod_map = (lambda b, i: (b, 0, 0)) if lm == 1 else (lambda b, i: (b, i, 0))
    n_out = len(ws)
    in_specs = [
        pl.BlockSpec((1, tl, D_MODEL), lambda b, i: (b, i, 0)),
        pl.BlockSpec((1, D_MODEL), lambda b, i: (0, 0)),
        pl.BlockSpec((1, tm, D_MODEL), mod_map),
        pl.BlockSpec((1, tm, D_MODEL), mod_map),
    ]
    in_specs += [pl.BlockSpec(w.shape, lambda b, i: (0, 0)) for w in ws]
    in_specs += [pl.BlockSpec(b.shape, lambda b, i: (0, 0)) for b in bs]
    return pl.pallas_call(
        functools.partial(_norm_proj_kernel, n_out),
        grid=(gsz, lg // tl),
        in_specs=in_specs,
        out_specs=[pl.BlockSpec((1, tl, w.shape[1]), lambda b, i: (b, i, 0)) for w in ws],
        out_shape=[jax.ShapeDtypeStruct((gsz, lg, w.shape[1]), F32) for w in ws],
        compiler_params=_cparams(("parallel", "parallel")),
        name="norm_proj",
    )(x3, g.reshape(1, D_MODEL), sh3, sc3, *ws, *bs)


def _out_proj_kernel(n_in, x_ref, gate_ref, *refs):
    o_refs, w_refs, y_ref = refs[:n_in], refs[n_in:2 * n_in], refs[2 * n_in]
    acc = None
    for o_ref, w_ref in zip(o_refs, w_refs):
        t = jnp.dot(o_ref[0].astype(BF16), w_ref[...], preferred_element_type=F32)
        acc = t if acc is None else acc + t
    y_ref[0] = x_ref[0] + gate_ref[0] * acc


def _out_proj(x3, gate3, os_, ws, tl):
    gsz, lg, _ = x3.shape
    lm = gate3.shape[1]
    tm = 1 if lm == 1 else tl
    mod_map = (lambda b, i: (b, 0, 0)) if lm == 1 else (lambda b, i: (b, i, 0))
    n_in = len(os_)
    in_specs = [
        pl.BlockSpec((1, tl, D_MODEL), lambda b, i: (b, i, 0)),
        pl.BlockSpec((1, tm, D_MODEL), mod_map),
    ]
    in_specs += [pl.BlockSpec((1, tl, o.shape[-1]), lambda b, i: (b, i, 0)) for o in os_]
    in_specs += [pl.BlockSpec(w.shape, lambda b, i: (0, 0)) for w in ws]
    return pl.pallas_call(
        functools.partial(_out_proj_kernel, n_in),
        grid=(gsz, lg // tl),
        in_specs=in_specs,
        out_specs=pl.BlockSpec((1, tl, D_MODEL), lambda b, i: (b, i, 0)),
        out_shape=jax.ShapeDtypeStruct(x3.shape, F32),
        compiler_params=_cparams(("parallel", "parallel")),
        name="out_proj",
    )(x3, gate3, *os_, *ws)


def _rot_half(x, lo32):
    return jnp.where(lo32, pltpu.roll(x, LANES - 32, 1), pltpu.roll(x, 32, 1))


def _retention_kernel(chunk, lgs_ref, q_ref, k_ref, v_ref, g_ref, cos_ref, sin_ref, gw_ref, gb_ref, s0_ref,
                      o_ref, s_ref, st_ref):
    npair = RET_HEADS // 2
    c = pl.program_id(1)

    @pl.when(c == 0)
    def _():
        st_ref[...] = s0_ref[0]

    lane = lax.broadcasted_iota(jnp.int32, (chunk, LANES), 1)
    lo = lane < (LANES // 2)
    lo32 = (lane % RET_DK) < (RET_DK // 2)
    cos = cos_ref[...]
    sin = sin_ref[...]
    ti = lax.broadcasted_iota(jnp.int32, (chunk, LANES), 0).astype(F32)
    ii = lax.broadcasted_iota(jnp.int32, (chunk, chunk), 0)
    jj = lax.broadcasted_iota(jnp.int32, (chunk, chunk), 1)
    diff = jnp.maximum(ii - jj, 0).astype(F32)
    causal = ii >= jj
    ri = lax.broadcasted_iota(jnp.int32, (LANES, LANES), 0)
    ci = lax.broadcasted_iota(jnp.int32, (LANES, LANES), 1)
    same_head = (ri < RET_DK) == (ci < RET_DK)

    def pair(p, st):
        sl = slice(p * LANES, (p + 1) * LANES)
        lg_a = lgs_ref[2 * p]
        lg_b = lgs_ref[2 * p + 1]
        lg = jnp.where(lo, lg_a, lg_b)
        q = q_ref[0, :, sl]
        k = k_ref[0, :, sl]
        q = q * cos + _rot_half(q, lo32) * sin
        k = (k * cos + _rot_half(k, lo32) * sin) * (RET_DK ** -0.5)
        kb = k.astype(BF16)
        vb = v_ref[0, :, sl].astype(BF16)

        def head(lg_h, sel):
            dmask = jnp.where(causal, jnp.exp(lg_h * diff), 0.0)
            sc = _dot_nt(jnp.where(sel, q, 0.0), kb) * dmask
            return _dot(sc, vb)

        o = jnp.where(lo, head(lg_a, lo), head(lg_b, jnp.logical_not(lo)))
        o = o + _dot(q * jnp.exp(lg * (ti + 1.0)), st)
        c_dec = jnp.exp(jnp.where(ri < RET_DK, lg_a, lg_b) * float(chunk))
        st_new = st * c_dec + jnp.where(same_head, _dot_tn(k * jnp.exp(lg * (chunk - 1.0 - ti)), vb), 0.0)
        mu = _pair_sum(o, lo) * (1.0 / RET_DK)
        d = o - mu
        var = _pair_sum(d * d, lo) * (1.0 / RET_DK)
        y = (d * lax.rsqrt(var + 1e-5)) * gw_ref[:, sl] + gb_ref[:, sl]
        return y * _silu(g_ref[0, :, sl]), st_new

    states = [st_ref[p] for p in range(npair)]
    results = [pair(p, states[p]) for p in range(npair)]
    for p, (y, st_new) in enumerate(results):
        o_ref[0, :, p * LANES:(p + 1) * LANES] = y
        st_ref[p] = st_new

    @pl.when(c == pl.num_programs(1) - 1)
    def _():
        for p, (_, st_new) in enumerate(results):
            s_ref[0, p] = st_new


def _retention(pa, cos_t, sin_t, lgs, gn_w, gn_b, s0_bd, chunk):
    bsz, seq, _ = pa.shape
    npair = RET_HEADS // 2
    nc = seq // chunk
    col = lambda j: pl.BlockSpec((1, chunk, RET_W), lambda b, c: (b, c, j))
    return pl.pallas_call(
        functools.partial(_retention_kernel, chunk),
        grid=(bsz, nc),
        in_specs=[
            pl.BlockSpec(memory_space=pltpu.SMEM),
            col(0), col(1), col(2), col(3),
            pl.BlockSpec((chunk, LANES), lambda b, c: (c, 0)),
            pl.BlockSpec((chunk, LANES), lambda b, c: (c, 0)),
            pl.BlockSpec((1, RET_W), lambda b, c: (0, 0)),
            pl.BlockSpec((1, RET_W), lambda b, c: (0, 0)),
            pl.BlockSpec((1, npair, LANES, LANES), lambda b, c: (b, 0, 0, 0)),
        ],
        out_specs=[
            pl.BlockSpec((1, chunk, RET_W), lambda b, c: (b, c, 0)),
            pl.BlockSpec((1, npair, LANES, LANES), lambda b, c: (b, 0, 0, 0)),
        ],
        out_shape=[
            jax.ShapeDtypeStruct((bsz, seq, RET_W), F32),
            jax.ShapeDtypeStruct((bsz, npair, LANES, LANES), F32),
        ],
        scratch_shapes=[pltpu.VMEM((npair, LANES, LANES), F32)],
        compiler_params=_cparams(("parallel", "arbitrary")),
        name="retention",
    )(lgs, pa, pa, pa, pa, cos_t, sin_t, gn_w.reshape(1, RET_W), gn_b.reshape(1, RET_W), s0_bd)


def _rwkv_kernel(nseq, tb, chunk, pb_ref, prev_ref, s0_ref, mu_ref, wlr_ref, w0_ref, a0_ref, kk_ref, ka_ref, rk_ref,
                 lnw_ref, lnb_ref, o_ref, s_ref, *scratch):
    npair = RWKV_HEADS // 2
    nchain = nseq * npair
    st_refs = scratch[:nchain]
    carry_ref, a_s, b_s, k_s, r_s, v_s, bp_s, kp_s, pc_s, y_s, tinv_s, lrb_s, lva_s, lvr_s = scratch[nchain:]
    blk = pl.program_id(1)

    @pl.when(blk == 0)
    def _():
        for s in range(nseq):
            for p in range(npair):
                st_refs[s * npair + p][...] = s0_ref[s, p]
            carry_ref[s] = prev_ref[s]

    lo_full = _lo_mask((tb, LANES))
    ti = lax.broadcasted_iota(jnp.int32, (tb, tb), 0)
    tj = lax.broadcasted_iota(jnp.int32, (tb, tb), 1)
    same_chunk = (ti // chunk) == (tj // chunk)
    sel = jnp.concatenate([jnp.logical_and(same_chunk, ti >= tj), same_chunk], axis=0).astype(F32)

    def prepare(s):
        rs = slice(s * tb, (s + 1) * tb)
        pb = pb_ref[s]
        row = lax.broadcasted_iota(jnp.int32, pb.shape, 0)
        prev = jnp.where(row == 0, carry_ref[s], pltpu.roll(pb, 1, 0))
        carry_ref[s] = pb[tb - 1:tb, :]
        pbs = pb + (prev - pb) * mu_ref[...]
        r = pbs[:, :RWKV_W]
        kb = pbs[:, RWKV_W:2 * RWKV_W]
        v = pbs[:, 2 * RWKV_W:3 * RWKV_W]
        tail = pbs[:, 3 * RWKV_W:]
        tl_lane = lax.broadcasted_iota(jnp.int32, tail.shape, 1)
        act = jnp.where(tl_lane < 64, jnp.tanh(tail), jnp.where(tl_lane < 128, tail, jax.nn.sigmoid(tail)))
        lr = _dot(act, wlr_ref[...])
        wz = -(w0_ref[...] + lr[:, :RWKV_W])
        softplus = jnp.maximum(wz, 0.0) + jnp.log1p(jnp.exp(-jnp.abs(wz)))
        logw = -jnp.exp(-softplus - 0.5)
        a = jax.nn.sigmoid(a0_ref[...] + lr[:, RWKV_W:2 * RWKV_W])
        kkf = kb * kk_ref[...]
        kb2 = kb * (1.0 + (a - 1.0) * ka_ref[...])
        bonus_src = r * kb2 * rk_ref[...]
        cums = _dot_exact_lhs(sel, logw)
        cum, tot = cums[:tb], cums[tb:]
        pin = jnp.exp(cum)
        pinv = jnp.exp(-cum)
        prem = jnp.exp(tot - cum)
        bonus = []
        for p in range(npair):
            sl = slice(p * LANES, (p + 1) * LANES)
            kf = kkf[:, sl]
            kn = kf * lax.rsqrt(jnp.maximum(_pair_sum(kf * kf, lo_full), 1e-24))
            bv = kn * a[:, sl]
            a_s[rs, sl] = -kn * jnp.exp(cum[:, sl] - logw[:, sl])
            b_s[rs, sl] = bv * pinv[:, sl]
            bp_s[rs, sl] = bv * prem[:, sl]
            bonus.append(_pair_sum(bonus_src[:, sl], lo_full) * v[:, sl])
        k_s[rs, :] = kb2 * pinv
        kp_s[rs, :] = kb2 * prem
        r_s[rs, :] = r * pin
        v_s[rs, :] = v
        pc_s[rs, :] = jnp.exp(tot)
        return bonus, lr[:, 2 * RWKV_W:]

    prepared = [prepare(s) for s in range(nseq)]

    c2 = 2 * chunk
    lo = _lo_mask((chunk, LANES))
    r2 = lax.broadcasted_iota(jnp.int32, (c2, c2), 0)
    q2 = lax.broadcasted_iota(jnp.int32, (c2, c2), 1)
    same = (r2 // chunk) == (q2 // chunk)
    strict = jnp.logical_and(same, (r2 % chunk) > (q2 % chunk))
    incl = jnp.logical_and(same, (r2 % chunk) >= (q2 % chunk))
    eye = (r2 == q2).astype(F32)
    n_sq = int(math.log2(chunk)) - 1
    pairs = range(nchain)
    lanes = [slice((c % npair) * LANES, (c % npair + 1) * LANES) for c in pairs]
    row0 = [(c // npair) * tb for c in pairs]

    def stack(x):
        return jnp.concatenate([jnp.where(lo, x, 0.0), jnp.where(lo, 0.0, x)], axis=0)

    def chunk_rows(ci):
        return [pl.ds(pl.multiple_of(row0[c] + ci * chunk, chunk), chunk) for c in pairs]

    def local_step(ci, carry):
        rows = chunk_rows(ci)
        ar = [jnp.concatenate([stack(a_s[rows[p], lanes[p]]), stack(r_s[rows[p], lanes[p]])], axis=0)
              for p in pairs]
        b2 = [b_s[rows[p], lanes[p]] for p in pairs]
        k2 = [k_s[rows[p], lanes[p]] for p in pairs]
        gb_ = [_dot_nt(ar[p], jnp.concatenate([b2[p], b2[p]], axis=0), hp=RWKV_HP_LOCAL) for p in pairs]
        gk_ = [_dot_nt(ar[p], jnp.concatenate([k2[p], k2[p]], axis=0), hp=RWKV_HP_LOCAL) for p in pairs]
        l_ab = [jnp.where(strict, gb_[p][:c2], 0.0) for p in pairs]
        tinv = [eye + l_ab[p] for p in pairs]
        xp = [_dot_x3(l_ab[p], l_ab[p]) for p in pairs]
        for _ in range(n_sq - 1):
            both = [_dot_x3(jnp.concatenate([tinv[p], xp[p]], axis=0), xp[p]) for p in pairs]
            tinv = [tinv[p] + both[p][:c2] for p in pairs]
            xp = [both[p][c2:] for p in pairs]
        tinv = [tinv[p] + _dot_x3(tinv[p], xp[p]) for p in pairs]
        for p in pairs:
            l_akrk = jnp.concatenate([jnp.where(strict, gk_[p][:c2], 0.0), jnp.where(incl, gk_[p][c2:], 0.0)], axis=0)
            lv = _dot(l_akrk, stack(v_s[rows[p], lanes[p]]), hp=RWKV_HP_LOCAL)
            tinv_s[ci, p] = tinv[p]
            lrb_s[ci, p] = jnp.where(incl, gb_[p][c2:], 0.0)
            lva_s[ci, p] = lv[:c2]
            lvr_s[ci, p] = lv[c2:]
        return carry

    lax.fori_loop(0, tb // chunk, local_step, 0)

    def state_step(ci, carry):
        rows = chunk_rows(ci)
        st = [st_refs[p][...] for p in pairs]
        ar = [jnp.concatenate([stack(a_s[rows[p], lanes[p]]), stack(r_s[rows[p], lanes[p]])], axis=0) for p in pairs]
        ars = [_dot_nt(ar[p], st[p], hp=RWKV_HP_STATE) for p in pairs]
        u_st = [_dot(tinv_s[ci, p], ars[p][:c2] + lva_s[ci, p], hp=RWKV_HP_STATE) for p in pairs]
        y_st = [ars[p][c2:] + lvr_s[ci, p] + _dot(lrb_s[ci, p], u_st[p], hp=RWKV_HP_STATE) for p in pairs]
        new = []
        for p in pairs:
            r, s = rows[p], lanes[p]
            uv = jnp.concatenate([u_st[p], stack(v_s[r, s])], axis=0)
            bk = jnp.concatenate([stack(bp_s[r, s]), stack(kp_s[r, s])], axis=0)
            pc = pc_s[pl.ds(row0[p] + ci * chunk, 1), s]
            new.append(st[p] * pc + _dot_tn(uv, bk, hp=RWKV_HP_STATE))
        for p in pairs:
            y_s[rows[p], lanes[p]] = y_st[p][:chunk] + y_st[p][chunk:]
            st_refs[p][...] = new[p]
        return carry

    lax.fori_loop(0, tb // chunk, state_step, 0)

    for s in range(nseq):
        bonus, gate = prepared[s]
        for p in range(npair):
            sl = slice(p * LANES, (p + 1) * LANES)
            y = y_s[s * tb:(s + 1) * tb, sl]
            mu = _pair_sum(y, lo_full) * (1.0 / RWKV_N)
            d = y - mu
            var = _pair_sum(d * d, lo_full) * (1.0 / RWKV_N)
            yn = (d * lax.rsqrt(var + 64e-5)) * lnw_ref[:, sl] + lnb_ref[:, sl]
            o_ref[s, :, sl] = (yn + bonus[p]) * gate[:, sl]

    @pl.when(blk == pl.num_programs(1) - 1)
    def _():
        for s in range(nseq):
            for p in range(npair):
                s_ref[s, p] = st_refs[s * npair + p][...]


def _rwkv(pb, prev, s0_bd, mu, wlr, w0, a0, k_k, k_a, r_k, ln_w, ln_b, tb, chunk):
    bsz, seq, _ = pb.shape
    npair = RWKV_HEADS // 2
    vec = lambda n: pl.BlockSpec((1, n), lambda b, i: (0, 0))
    nseq = RWKV_NSEQ
    nchain = nseq * npair
    scr = lambda: pltpu.VMEM((nseq * tb, RWKV_W), F32)
    nch, c2 = tb // chunk, 2 * chunk
    return pl.pallas_call(
        functools.partial(_rwkv_kernel, nseq, tb, chunk),
        grid=(bsz // nseq, seq // tb),
        in_specs=[
            pl.BlockSpec((nseq, tb, B_COLS), lambda b, i: (b, i, 0)),
            pl.BlockSpec((nseq, 1, B_COLS), lambda b, i: (b, 0, 0)),
            pl.BlockSpec((nseq, npair, LANES, LANES), lambda b, i: (b, 0, 0, 0)),
            vec(B_COLS),
            pl.BlockSpec(wlr.shape, lambda b, i: (0, 0)),
            vec(RWKV_W), vec(RWKV_W), vec(RWKV_W), vec(RWKV_W), vec(RWKV_W), vec(RWKV_W), vec(RWKV_W),
        ],
        out_specs=[
            pl.BlockSpec((nseq, tb, RWKV_W), lambda b, i: (b, i, 0)),
            pl.BlockSpec((nseq, npair, LANES, LANES), lambda b, i: (b, 0, 0, 0)),
        ],
        out_shape=[
            jax.ShapeDtypeStruct((bsz, seq, RWKV_W), F32),
            jax.ShapeDtypeStruct((bsz, npair, LANES, LANES), F32),
        ],
        scratch_shapes=[pltpu.VMEM((LANES, LANES), F32) for _ in range(nchain)] + [
            pltpu.VMEM((nseq, 1, B_COLS), F32),
            scr(), scr(), scr(), scr(), scr(), scr(), scr(), scr(), scr(),
            pltpu.VMEM((nch, nchain, c2, c2), F32), pltpu.VMEM((nch, nchain, c2, c2), F32),
            pltpu.VMEM((nch, nchain, c2, LANES), F32), pltpu.VMEM((nch, nchain, c2, LANES), F32),
        ],
        compiler_params=_cparams(("parallel", "arbitrary")),
        name="rwkv7",
    )(pb, prev, s0_bd, mu.reshape(1, -1), wlr, w0.reshape(1, -1), a0.reshape(1, -1), k_k.reshape(1, -1),
      k_a.reshape(1, -1), r_k.reshape(1, -1), ln_w.reshape(1, -1), ln_b.reshape(1, -1))


def _sb_block(z, mask, carry, tri_u):
    log_w, rem_sum = _sb_block_logs(z, mask, tri_u)
    return _sb_block_weights(log_w, carry, mask), carry + rem_sum


def _sb_block_logs(z, mask, tri_u):
    log_beta = jnp.minimum(z, 0.0) - jnp.log2(1.0 + jnp.exp2(-jnp.abs(z)))
    log_rem = log_beta - z
    if mask is not None:
        log_rem = jnp.where(mask, log_rem, 0.0)
    hi = log_rem.astype(BF16)
    lo = (log_rem - hi.astype(F32)).astype(BF16)
    after = jnp.dot(jnp.concatenate([hi, lo], axis=1), tri_u, preferred_element_type=F32)
    return log_beta + after, jnp.sum(log_rem, axis=-1, keepdims=True)


def _sb_block_weights(log_w, carry, mask):
    w = jnp.exp2(log_w + carry)
    return w if mask is None else jnp.where(mask, w, 0.0)


def _strict_upper(n):
    r = lax.broadcasted_iota(jnp.int32, (2 * n, n), 0) % n
    c = lax.broadcasted_iota(jnp.int32, (2 * n, n), 1)
    return (r > c).astype(BF16)


def _sb_prompt_kernel(tq, tk, q_ref, k_ref, v_ref, o_ref):
    qi = pl.program_id(2)
    q = q_ref[0] * SB_SCALE
    lo = _lo_mask((tq, LANES))
    qa = jnp.where(lo, q, 0.0).astype(BF16)
    qb = jnp.where(lo, 0.0, q).astype(BF16)
    tri_u = _strict_upper(tk)
    qpos = qi * tq + lax.broadcasted_iota(jnp.int32, (tq, tk), 0)
    kofs = lax.broadcasted_iota(jnp.int32, (tq, tk), 1)
    n_diag = tq // tk
    n_full = qi * n_diag

    def visit(kb_idx, state, masked):
        acc_a, acc_b, car_a, car_b = state
        rows = pl.ds(pl.multiple_of(kb_idx * tk, tk), tk)
        kblk = k_ref[0, rows, :].astype(BF16)
        vblk = v_ref[0, rows, :].astype(BF16)
        mask = ((kb_idx * tk + kofs) < qpos) if masked else None
        w_a, car_a = _sb_block(_dot_nt(qa, kblk), mask, car_a, tri_u)
        w_b, car_b = _sb_block(_dot_nt(qb, kblk), mask, car_b, tri_u)
        acc_a = acc_a + jnp.dot(w_a.astype(BF16), vblk, preferred_element_type=F32)
        acc_b = acc_b + jnp.dot(w_b.astype(BF16), vblk, preferred_element_type=F32)
        return acc_a, acc_b, car_a, car_b

    zero = jnp.zeros((tq, LANES), F32)
    zc = jnp.zeros((tq, 1), F32)
    state = (zero, zero, zc, zc)
    for d in reversed(range(n_diag)):
        state = visit(n_full + d, state, True)
    state = lax.fori_loop(0, n_full, lambda j, s: visit(n_full - 1 - j, s, False), state)
    o_ref[0] = jnp.where(lo, state[0], state[1])


def _sb_prompt(q, k, v, tq, tk):
    bsz, seq, _ = q.shape
    npair = SB_HEADS // 2
    return pl.pallas_call(
        functools.partial(_sb_prompt_kernel, tq, tk),
        grid=(bsz, npair, seq // tq),
        in_specs=[
            pl.BlockSpec((1, tq, LANES), lambda b, h, i: (b, i, h)),
            pl.BlockSpec((1, seq, LANES), lambda b, h, i: (b, 0, h)),
            pl.BlockSpec((1, seq, LANES), lambda b, h, i: (b, 0, h)),
        ],
        out_specs=pl.BlockSpec((1, tq, LANES), lambda b, h, i: (b, i, h)),
        out_shape=jax.ShapeDtypeStruct((bsz, seq, SB_W), F32),
        compiler_params=_cparams(("parallel", "parallel", "arbitrary")),
        name="sb_prompt",
    )(q, k, v)


def _sb_sample_kernel(npg, lq, *refs):
    pt_ref = refs[0]
    q_ref, kn_ref, vn_ref = refs[1:4]
    kp_refs = refs[4:4 + npg]
    vp_refs = refs[4 + npg:4 + 2 * npg]
    o_ref = refs[4 + 2 * npg]
    qbd_ref, acc_ref, car_ref = refs[5 + 2 * npg:]
    del pt_ref
    j = pl.program_id(1)
    rows = SB_HEADS * lq
    tri_u = _strict_upper(PAGE)

    def visit_new(kblk, vblk, mask):
        w, car = _sb_block(_dot_nt(qbd_ref[...], kblk), mask, car_ref[...], tri_u)
        car_ref[...] = car
        acc_ref[...] += _dot(w, vblk)

    @pl.when(j == 0)
    def _():
        q = q_ref[0] * SB_SCALE
        qt = jnp.concatenate([q] * SB_HEADS, axis=0)
        rh = lax.broadcasted_iota(jnp.int32, (rows, SB_W), 0) // lq
        ch = lax.broadcasted_iota(jnp.int32, (rows, SB_W), 1) // SB_DH
        qbd_ref[...] = jnp.where(rh == ch, qt, 0.0).astype(BF16)
        acc_ref[...] = jnp.zeros_like(acc_ref)
        car_ref[...] = jnp.zeros_like(car_ref)
        pad = jnp.zeros((PAGE - lq, SB_W), F32)
        kblk = jnp.concatenate([kn_ref[0], pad], axis=0)
        vblk = jnp.concatenate([vn_ref[0], pad], axis=0)
        qidx = lax.broadcasted_iota(jnp.int32, (rows, PAGE), 0) % lq
        kidx = lax.broadcasted_iota(jnp.int32, (rows, PAGE), 1)
        visit_new(kblk, vblk, kidx < qidx)

    k_all = jnp.concatenate([kp_ref[0, 0].astype(BF16) for kp_ref in kp_refs], axis=1)
    z = jnp.dot(qbd_ref[...], k_all, preferred_element_type=F32)
    log_beta = jnp.minimum(z, 0.0) - jnp.log2(1.0 + jnp.exp2(-jnp.abs(z)))
    log_rem = log_beta - z
    hi = log_rem.astype(BF16)
    lo = (log_rem - hi.astype(F32)).astype(BF16)
    page = lambda t, p: t[:, p * PAGE:(p + 1) * PAGE]
    split = jnp.concatenate([jnp.concatenate([page(hi, p), page(lo, p)], axis=1) for p in range(npg)], axis=0)
    after = jnp.dot(split, tri_u, preferred_element_type=F32)
    car = car_ref[...]
    ws = []
    for p in range(npg):
        ws.append(jnp.exp2(page(log_beta, p) + after[p * rows:(p + 1) * rows] + car).astype(BF16))
        car = car + jnp.sum(page(log_rem, p), axis=-1, keepdims=True)
    car_ref[...] = car
    v_all = jnp.concatenate([vp_ref[0, 0].astype(BF16) for vp_ref in vp_refs], axis=1)
    acc_ref[...] += _dot_nt(jnp.concatenate(ws, axis=1), v_all)

    @pl.when(j == pl.num_programs(1) - 1)
    def _():
        acc = acc_ref[...]
        ch = lax.broadcasted_iota(jnp.int32, (lq, SB_W), 1) // SB_DH
        out = jnp.zeros((lq, SB_W), F32)
        for h in range(SB_HEADS):
            out = out + jnp.where(ch == h, acc[h * lq:(h + 1) * lq, :], 0.0)
        o_ref[0] = out


def _sb_sample(q, k_new, v_new, cache_k, cache_v, page_table, npg):
    bsz, lq, _ = q.shape
    n_pages = page_table.shape[1]
    rows = SB_HEADS * lq
    tok = pl.BlockSpec((1, lq, SB_W), lambda b, j, pt: (b, 0, 0))

    def page_spec(i):
        return pl.BlockSpec((1, 1, SB_W, PAGE), lambda b, j, pt: (0, pt[b, n_pages - 1 - (j * npg + i)], 0, 0))

    grid_spec = pltpu.PrefetchScalarGridSpec(
        num_scalar_prefetch=1,
        grid=(bsz, n_pages // npg),
        in_specs=[tok, tok, tok] + [page_spec(i) for i in range(npg)] * 2,
        out_specs=pl.BlockSpec((1, lq, SB_W), lambda b, j, pt: (b, 0, 0)),
        scratch_shapes=[
            pltpu.VMEM((rows, SB_W), BF16),
            pltpu.VMEM((rows, SB_W), F32),
            pltpu.VMEM((rows, 1), F32),
        ],
    )
    return pl.pallas_call(
        functools.partial(_sb_sample_kernel, npg, lq),
        grid_spec=grid_spec,
        out_shape=jax.ShapeDtypeStruct((bsz, lq, SB_W), F32),
        compiler_params=_cparams(("parallel", "arbitrary")),
        name="sb_sample",
    )(page_table, q, k_new, v_new, *([cache_k] * npg), *([cache_v] * npg))


def _first_index(cond, idx, big):
    return jnp.min(jnp.where(cond, idx, big), axis=(0, 1), keepdims=True)


def _router_kernel(x_ref, g_ref, sh_ref, sc_ref, rw_ref, rb_ref, h_ref, comb_ref, cnt_ref):
    h = _modulated_norm(x_ref[0], g_ref[...], sh_ref[0], sc_ref[0])
    h_ref[0] = h.astype(BF16)
    tl = h.shape[0]
    logits = _dot_nt(rw_ref[...], h, hp=True)
    scores = jax.nn.sigmoid(logits).reshape(N_GROUPS, GROUP_SIZE, tl)
    biased = scores + rb_ref[...].reshape(N_GROUPS, GROUP_SIZE, 1)
    neg = -jnp.inf

    jidx = lax.broadcasted_iota(jnp.int32, biased.shape, 1)
    m1 = jnp.max(biased, axis=1, keepdims=True)
    first = jnp.min(jnp.where(biased == m1, jidx, GROUP_SIZE), axis=1, keepdims=True)
    m2 = jnp.max(jnp.where(jidx == first, neg, biased), axis=1, keepdims=True)
    gscore = m1 + m2

    gidx = lax.broadcasted_iota(jnp.int32, gscore.shape, 0)
    gsel = jnp.zeros(gscore.shape, jnp.bool_)
    for _ in range(TOPK_GROUPS):
        m = jnp.max(gscore, axis=0, keepdims=True)
        pick = gidx == jnp.min(jnp.where(gscore == m, gidx, N_GROUPS), axis=0, keepdims=True)
        gsel = jnp.logical_or(gsel, pick)
        gscore = jnp.where(pick, neg, gscore)

    cand = jnp.where(gsel, biased, neg)
    eidx = lax.broadcasted_iota(jnp.int32, cand.shape, 0) * GROUP_SIZE + jidx
    esel = jnp.zeros(cand.shape, jnp.bool_)
    for _ in range(TOP_K):
        m = jnp.max(cand, axis=(0, 1), keepdims=True)
        pick = eidx == _first_index(cand == m, eidx, N_EXPERTS)
        esel = jnp.logical_or(esel, pick)
        cand = jnp.where(pick, neg, cand)

    sel = jnp.where(esel, scores, 0.0)
    wts = (sel / jnp.sum(sel, axis=(0, 1), keepdims=True) * ROUTED_SCALE).reshape(N_EXPERTS, tl)
    comb_ref[0] = wts
    cnt = jnp.sum((wts > 0.0).astype(jnp.int32), axis=-1, keepdims=True)
    cnt_ref[0] = jnp.broadcast_to(cnt, (N_EXPERTS, LANES))


def _router(x3, g, sh3, sc3, router_w, router_bias):
    gsz, lg, _ = x3.shape
    tl = MOE_TB
    tpg = lg // tl
    lm = sh3.shape[1]
    tm = 1 if lm == 1 else tl
    mod_map = (lambda b, i: (b, 0, 0)) if lm == 1 else (lambda b, i: (b, i, 0))
    return pl.pallas_call(
        _router_kernel,
        grid=(gsz, tpg),
        in_specs=[
            pl.BlockSpec((1, tl, D_MODEL), lambda b, i: (b, i, 0)),
            pl.BlockSpec((1, D_MODEL), lambda b, i: (0, 0)),
            pl.BlockSpec((1, tm, D_MODEL), mod_map),
            pl.BlockSpec((1, tm, D_MODEL), mod_map),
            pl.BlockSpec((N_EXPERTS, D_MODEL), lambda b, i: (0, 0)),
            pl.BlockSpec((N_EXPERTS, 1), lambda b, i: (0, 0)),
        ],
        out_specs=[
            pl.BlockSpec((1, tl, D_MODEL), lambda b, i: (b, i, 0)),
            pl.BlockSpec((1, N_EXPERTS, tl), lambda b, i: (b * tpg + i, 0, 0)),
            pl.BlockSpec((1, N_EXPERTS, LANES), lambda b, i: (b * tpg + i, 0, 0)),
        ],
        out_shape=[
            jax.ShapeDtypeStruct((gsz, lg, D_MODEL), BF16),
            jax.ShapeDtypeStruct((gsz * tpg, N_EXPERTS, tl), F32),
            jax.ShapeDtypeStruct((gsz * tpg, N_EXPERTS, LANES), jnp.int32),
        ],
        compiler_params=_cparams(("parallel", "parallel")),
        name="moe_router",
    )(x3, g.reshape(1, D_MODEL), sh3, sc3, router_w.T, router_bias.reshape(N_EXPERTS, 1))


def _swiglu_hidden(h, wg, wu):
    return _silu(jnp.dot(h, wg, preferred_element_type=F32)) * jnp.dot(h, wu, preferred_element_type=F32)


def _moe_plan(cnt, nt_max):
    e_ids = N_EXPERTS
    pc = (cnt + MOE_UNIT - 1) // MOE_UNIT * MOE_UNIT
    lend = jnp.cumsum(pc, axis=1)
    lstart = lend - pc
    nun = (lend[:, -1] // MOE_UNIT).astype(jnp.int32)
    ctot = jnp.sum(pc, axis=0)
    rs = (ctot + MOE_RT - 1) // MOE_RT * MOE_RT
    gend = jnp.cumsum(rs)
    gstart = gend - rs
    toff = jnp.cumsum(pc, axis=0) - pc
    urow = jnp.arange(MOE_NU, dtype=jnp.int32) * MOE_UNIT
    eu = jnp.sum((urow[None, :, None] >= lend[:, None, :]).astype(jnp.int32), axis=-1)
    valid = eu < e_ids
    run_start = jnp.max(jnp.where(lstart[:, None, :] <= urow[None, :, None], lstart[:, None, :], 0), axis=-1)
    uinfo = jnp.where(valid, eu * MOE_INFO_SHIFT + (urow[None, :] - run_start), e_ids * MOE_INFO_SHIFT)
    gbase = (gstart[None, :] + toff) // MOE_UNIT
    zn = ((rs - ctot) // MOE_UNIT).astype(jnp.int32)
    zdst = (gstart + ctot)[:, None] // MOE_UNIT + jnp.arange(MOE_RT // MOE_UNIT, dtype=jnp.int32)[None, :]
    n_used = (gend[-1] // MOE_RT).astype(jnp.int32).reshape(1)
    tmap = jnp.sum((jnp.arange(nt_max, dtype=jnp.int32)[:, None] * MOE_RT >= gend[None, :]).astype(jnp.int32), axis=-1)
    tmap = jnp.minimum(tmap, e_ids - 1)
    i32 = lambda a: a.astype(jnp.int32).reshape(-1)
    return dict(uinfo=i32(uinfo), gbase=i32(gbase),nun=nun, zn=zn, zdst=i32(zdst), tmap=i32(tmap), n_used=n_used)


def _routing_rank(sel, rank_s, wsel_s=None):
    tb = sel.shape[1]
    mask = sel > 0.0
    before = (lax.broadcasted_iota(jnp.int32, (tb, tb), 0) < lax.broadcasted_iota(jnp.int32, (tb, tb), 1))
    rank = jnp.dot(mask.astype(BF16), before.astype(BF16), preferred_element_type=F32)
    rank_s[0:N_EXPERTS, :] = jnp.where(mask, rank, -1.0)
    rank_s[N_EXPERTS:, :] = jnp.full((8, tb), -1.0, F32)
    if wsel_s is not None:
        wsel_s[0:N_EXPERTS, :] = sel
        wsel_s[N_EXPERTS:, :] = jnp.zeros((8, tb), F32)


def _unit_rows(uinfo_ref, gbase_ref, i, u, rank_s, gunit_s):
    info = uinfo_ref[i * MOE_NU + u]
    e = lax.shift_right_logical(info, MOE_INFO_BITS)
    off = jnp.bitwise_and(info, MOE_INFO_SHIFT - 1)
    gunit_s[u] = (gbase_ref[i * N_EXPERTS + jnp.minimum(e, N_EXPERTS - 1)]
                  + lax.shift_right_logical(off, MOE_UNIT_BITS))
    j = lax.broadcasted_iota(jnp.int32, (MOE_UNIT, MOE_TB), 0).astype(F32)
    return rank_s[pl.ds(e, 1), :] == (j + off.astype(F32)), e


def _for_each_unit(n, fn):
    def group(g, c):
        for k in range(MOE_ISSUE):
            fn(g * MOE_ISSUE + k)
        return c

    def single(u, c):
        fn(u)
        return c

    full = lax.shift_right_logical(n, MOE_ISSUE_BITS)
    lax.fori_loop(0, full, group, 0)
    lax.fori_loop(full * MOE_ISSUE, n, single, 0)


def _wait_units(n, src, dst, sem):
    @pl.when(n > 0)
    def _():
        rows = pl.ds(0, n * MOE_UNIT)
        pltpu.make_async_copy(src.at[rows], dst.at[rows], sem).wait()


def _unit_copy(src, dst, sem):
    return pltpu.make_async_copy(src, dst, sem)


def _moe_dispatch_kernel(uinfo_ref, gbase_ref, nun_ref, zn_ref, zdst_ref, h_ref, sel_ref, xs_hbm,
                         p_s, xs_s0, xs_s1, rank_s, zero_s, gunit_s, sem0, sem1, zsem):
    i = pl.program_id(0)
    n = nun_ref[i]
    unit = lambda ref, u: ref.at[pl.ds(pl.multiple_of(u * MOE_UNIT, MOE_UNIT), MOE_UNIT)]

    @pl.when(i == 0)
    def _():
        zero_s[...] = jnp.zeros_like(zero_s)
        zper = MOE_RT // MOE_UNIT

        def per_expert(e, tot):
            def one(z, c):
                _unit_copy(zero_s, unit(xs_hbm, zdst_ref[e * zper + z]), zsem).start()
                return c
            lax.fori_loop(0, zn_ref[e], one, 0)
            return tot + zn_ref[e]

        total = lax.fori_loop(0, N_EXPERTS, per_expert, 0)

        def wait_one(z, c):
            _unit_copy(zero_s, unit(xs_hbm, 0), zsem).wait()
            return c
        lax.fori_loop(0, total, wait_one, 0)

    _routing_rank(sel_ref[0], rank_s)
    for u in range(MOE_NU):
        hit, _ = _unit_rows(uinfo_ref, gbase_ref, i, u, rank_s, gunit_s)
        p_s[u * MOE_UNIT:(u + 1) * MOE_UNIT, :] = hit.astype(BF16)

    h = h_ref[0]
    last = pl.num_programs(0) - 1

    def permute_and_send(xs_s, sem, other_s, other_sem):
        @pl.when(i >= 2)
        def _():
            _wait_units(nun_ref[jnp.maximum(i - 2, 0)], xs_s, xs_hbm, sem)

        for c in range(MOE_RMAX // MOE_CH):
            @pl.when(c * (MOE_CH // MOE_UNIT) < n)
            def _():
                rows = slice(c * MOE_CH, (c + 1) * MOE_CH)
                xs_s[rows, :] = jnp.dot(p_s[rows, :], h, preferred_element_type=F32).astype(BF16)

        _for_each_unit(n, lambda u: _unit_copy(unit(xs_s, u), unit(xs_hbm, gunit_s[u]), sem).start())

        @pl.when(i == last)
        def _():
            _wait_units(n, xs_s, xs_hbm, sem)

            @pl.when(i >= 1)
            def _():
                _wait_units(nun_ref[jnp.maximum(i - 1, 0)], other_s, xs_hbm, other_sem)

    @pl.when(jnp.bitwise_and(i, 1) == 0)
    def _():
        permute_and_send(xs_s0, sem0, xs_s1, sem1)

    @pl.when(jnp.bitwise_and(i, 1) == 1)
    def _():
        permute_and_send(xs_s1, sem1, xs_s0, sem0)


def _moe_dispatch(h3, sel, plan, pmax):
    gsz, lg, _ = h3.shape
    tpg = lg // MOE_TB
    n_tiles = gsz * tpg
    grid_spec = pltpu.PrefetchScalarGridSpec(
        num_scalar_prefetch=5,
        grid=(n_tiles,),
        in_specs=[
            pl.BlockSpec((1, MOE_TB, D_MODEL), lambda i, *_: (i // tpg, i % tpg, 0)),
            pl.BlockSpec((1, N_EXPERTS, MOE_TB), lambda i, *_: (i, 0, 0)),
        ],
        out_specs=pl.BlockSpec(memory_space=pl.ANY),
        scratch_shapes=[
            pltpu.VMEM((MOE_RMAX, MOE_TB), BF16),
            pltpu.VMEM((MOE_RMAX, D_MODEL), BF16),
            pltpu.VMEM((MOE_RMAX, D_MODEL), BF16),
            pltpu.VMEM((N_EXPERTS + 8, MOE_TB), F32),
            pltpu.VMEM((MOE_UNIT, D_MODEL), BF16),
            pltpu.SMEM((MOE_NU,), jnp.int32),
            pltpu.SemaphoreType.DMA(()),
            pltpu.SemaphoreType.DMA(()),
            pltpu.SemaphoreType.DMA(()),
        ],
    )
    return pl.pallas_call(
        _moe_dispatch_kernel,
        grid_spec=grid_spec,
        out_shape=jax.ShapeDtypeStruct((pmax, D_MODEL), BF16),
        compiler_params=_cparams(("arbitrary",)),
        name="moe_dispatch",
    )(plan['uinfo'], plan['gbase'], plan['nun'], plan['zn'], plan['zdst'], h3, sel)


def _moe_group_kernel(tmap_ref, nused_ref, xs_ref, wg_ref, wu_ref, wd_ref, ys_ref, wg_s, wu_s, wd_s):
    j = pl.program_id(0)

    @pl.when(j < nused_ref[0])
    def _():
        @pl.when(jnp.logical_or(j == 0, tmap_ref[j] != tmap_ref[jnp.maximum(j - 1, 0)]))
        def _():
            wg_s[...] = wg_ref[0, 0].astype(BF16)
            wu_s[...] = wu_ref[0, 0].astype(BF16)
            wd_s[...] = wd_ref[0, 0].astype(BF16)

        act = _swiglu_hidden(xs_ref[...], wg_s[...], wu_s[...])
        ys_ref[...] = jnp.dot(act.astype(BF16), wd_s[...], preferred_element_type=F32).astype(BF16)


def _moe_group(xs, plan, wg, wu, wd, layer):
    pmax = xs.shape[0]
    nt = pmax // MOE_RT
    row_map = lambda j, tmap, nused: (jnp.maximum(jnp.minimum(j, nused[0] - 1), 0), 0)
    wmap = lambda j, tmap, nused: (layer, tmap[j], 0, 0)
    grid_spec = pltpu.PrefetchScalarGridSpec(
        num_scalar_prefetch=2,
        grid=(nt,),
        in_specs=[
            pl.BlockSpec((MOE_RT, D_MODEL), row_map),
            pl.BlockSpec((1, 1, D_MODEL, HIDDEN), wmap),
            pl.BlockSpec((1, 1, D_MODEL, HIDDEN), wmap),
            pl.BlockSpec((1, 1, HIDDEN, D_MODEL), wmap),
        ],
        out_specs=pl.BlockSpec((MOE_RT, D_MODEL), row_map),
        scratch_shapes=[
            pltpu.VMEM((D_MODEL, HIDDEN), BF16),
            pltpu.VMEM((D_MODEL, HIDDEN), BF16),
            pltpu.VMEM((HIDDEN, D_MODEL), BF16),
        ],
    )
    return pl.pallas_call(
        _moe_group_kernel,
        grid_spec=grid_spec,
        out_shape=jax.ShapeDtypeStruct((pmax, D_MODEL), BF16),
        compiler_params=_cparams(("arbitrary",)),
        name="moe_group",
    )(plan['tmap'], plan['n_used'], xs, wg, wu, wd)


def _moe_combine_kernel(final, uinfo_ref, gbase_ref, nun_ref, h_ref, sel_ref, x_ref, gate_ref, fn_ref,
                        sg_ref, su_ref, sd_ref, ys_hbm, o_ref, pw_s, ys_s, rank_s, wsel_s, acc_s, gunit_s, sem):
    i = pl.program_id(0)
    n = nun_ref[i]
    unit = lambda ref, u: ref.at[pl.ds(pl.multiple_of(u * MOE_UNIT, MOE_UNIT), MOE_UNIT)]

    @pl.when(i == 0)
    def _():
        ys_s[...] = jnp.zeros_like(ys_s)

    _routing_rank(sel_ref[0], rank_s, wsel_s)
    for u in range(MOE_NU):
        hit, e = _unit_rows(uinfo_ref, gbase_ref, i, u, rank_s, gunit_s)
        pw_s[u * MOE_UNIT:(u + 1) * MOE_UNIT, :] = jnp.where(hit, wsel_s[pl.ds(e, 1), :], 0.0).astype(BF16)

    _for_each_unit(n, lambda u: _unit_copy(unit(ys_hbm, gunit_s[u]), unit(ys_s, u), sem).start())

    h = h_ref[0]
    act = _swiglu_hidden(h, sg_ref[...], su_ref[...])
    acc_s[...] = jnp.dot(act.astype(BF16), sd_ref[...], preferred_element_type=F32)

    _wait_units(n, ys_hbm, ys_s, sem)

    for c in range(MOE_RMAX // MOE_CH):
        @pl.when(c * (MOE_CH // MOE_UNIT) < n)
        def _():
            rows = slice(c * MOE_CH, (c + 1) * MOE_CH)
            acc_s[...] += _dot_tn(pw_s[rows, :], ys_s[rows, :])

    y = x_ref[0] + gate_ref[0] * acc_s[...]
    if final:
        y = (y * lax.rsqrt(jnp.mean(y * y, axis=-1, keepdims=True) + EPS)) * fn_ref[...]
    o_ref[0] = y


def _moe_combine(x3, gate3, h3, sel, ys, plan, sg, su, sd, final_norm, final):
    gsz, lg, _ = x3.shape
    tpg = lg // MOE_TB
    lm = gate3.shape[1]
    tm = 1 if lm == 1 else MOE_TB
    tok_map = lambda i, *_: (i // tpg, i % tpg, 0)
    mod_map = (lambda i, *_: (i // tpg, 0, 0)) if lm == 1 else tok_map
    full = lambda a: pl.BlockSpec(a.shape, lambda i, *_: (0, 0))
    grid_spec = pltpu.PrefetchScalarGridSpec(
        num_scalar_prefetch=3,
        grid=(gsz * tpg,),
        in_specs=[
            pl.BlockSpec((1, MOE_TB, D_MODEL), tok_map),
            pl.BlockSpec((1, N_EXPERTS, MOE_TB), lambda i, *_: (i, 0, 0)),
            pl.BlockSpec((1, MOE_TB, D_MODEL), tok_map),
            pl.BlockSpec((1, tm, D_MODEL), mod_map),
            pl.BlockSpec((1, D_MODEL), lambda i, *_: (0, 0)),
            full(sg), full(su), full(sd),
            pl.BlockSpec(memory_space=pl.ANY),
        ],
        out_specs=pl.BlockSpec((1, MOE_TB, D_MODEL), tok_map),
        scratch_shapes=[
            pltpu.VMEM((MOE_RMAX, MOE_TB), BF16),
            pltpu.VMEM((MOE_RMAX, D_MODEL), BF16),
            pltpu.VMEM((N_EXPERTS + 8, MOE_TB), F32),
            pltpu.VMEM((N_EXPERTS + 8, MOE_TB), F32),
            pltpu.VMEM((MOE_TB, D_MODEL), F32),
            pltpu.SMEM((MOE_NU,), jnp.int32),
            pltpu.SemaphoreType.DMA(()),
        ],
    )
    return pl.pallas_call(
        functools.partial(_moe_combine_kernel, final),
        grid_spec=grid_spec,
        out_shape=jax.ShapeDtypeStruct(x3.shape, F32),
        compiler_params=_cparams(("arbitrary",)),
        name="moe_combine",
    )(plan['uinfo'], plan['gbase'], plan['nun'], h3, sel, x3, gate3, final_norm.reshape(1, D_MODEL), sg, su, sd, ys)


def _block_diag_pairs(s):
    bsz, nh, n, _ = s.shape
    s = s.reshape(bsz, nh // 2, 2, n, n)
    z = jnp.zeros_like(s[:, :, 0])
    top = jnp.concatenate([s[:, :, 0], z], axis=-1)
    bot = jnp.concatenate([z, s[:, :, 1]], axis=-1)
    return jnp.concatenate([top, bot], axis=-2)


def _unpair(s_bd):
    bsz, npair, n2, _ = s_bd.shape
    n = n2 // 2
    return jnp.stack([s_bd[:, :, :n, :n], s_bd[:, :, n:, n:]], axis=2).reshape(bsz, npair * 2, n, n)


def _rotary_tables(pos):
    half = RET_DK // 2
    inv = ROPE_BASE ** (-jnp.arange(half, dtype=F32) / half)
    ang = pos.astype(F32)[:, None] * inv[None, :]
    cos, sin = jnp.cos(ang), jnp.sin(ang)
    cos_t = jnp.concatenate([cos, cos, cos, cos], axis=-1)
    sin_t = jnp.concatenate([-sin, sin, -sin, sin], axis=-1)
    return cos_t, sin_t


def _trunk(x, c, pos, ret_s, rwkv_s, shift_s, cache, p, flat):
    bsz, seq, _ = x.shape
    mod = _ada(c, p['ada_w'], p['ada_b'])
    mod = mod.reshape(DEPTH, bsz, 6, D_MODEL)
    if flat:
        tok = lambda t: t.reshape(1, bsz * seq, t.shape[-1])
        untok = lambda t: t.reshape(bsz, seq, t.shape[-1])
        modv = lambda l, j: jnp.broadcast_to(mod[l, :, j][:, None, :], (bsz, seq, D_MODEL)).reshape(1, bsz * seq, D_MODEL)
        tl_proj = bsz * seq
    else:
        tok = untok = lambda t: t
        modv = lambda l, j: mod[l, :, j][:, None, :]
        tl_proj = min(256, seq)

    x3 = tok(x)
    w_in = p['w_in_ab'][0].astype(BF16)
    a_cols = 4 * RET_W
    zeros_a = jnp.zeros((1, a_cols), F32)
    zeros_b = jnp.zeros((1, B_COLS), F32)
    pa, pb = _norm_proj(x3, p['norm_mix'][0], modv(0, 0), modv(0, 1), [w_in[:, :a_cols], w_in[:, a_cols:]],
                        [zeros_a, zeros_b], tl_proj)
    pa, pb = untok(pa), untok(pb)
    cos_t, sin_t = _rotary_tables(pos)
    lgs = jnp.log1p(-jnp.exp2(-5.0 - jnp.arange(RET_HEADS, dtype=F32)))
    ret_chunk = math.gcd(seq, 128)
    o_a, ret_new = _retention(pa, cos_t, sin_t, lgs, p['ret_gn_w'][0], p['ret_gn_b'][0],
                              _block_diag_pairs(ret_s), ret_chunk)
    wlr = jnp.zeros((LOWRANK, 3 * RWKV_W), F32)
    wlr = wlr.at[0:64, 0:RWKV_W].set(p['rwkv_w_up'][0])
    wlr = wlr.at[64:128, RWKV_W:2 * RWKV_W].set(p['rwkv_a_up'][0])
    wlr = wlr.at[128:256, 2 * RWKV_W:].set(p['rwkv_g_up'][0])
    tb = min(seq, 128)
    o_b, rwkv_new = _rwkv(pb, shift_s[:, None, :], _block_diag_pairs(rwkv_s), p['rwkv_mu'][0], wlr.astype(BF16),
                          p['rwkv_w0'][0], p['rwkv_a0'][0], p['rwkv_k_k'][0], p['rwkv_k_a'][0],
                          p['rwkv_r_k'][0].reshape(-1), p['rwkv_ln_w'][0], p['rwkv_ln_b'][0],
                          tb, min(RWKV_CHUNK, seq))
    w_out = p['w_out_ab'][0].astype(BF16)
    x3 = _out_proj(x3, modv(0, 2), [tok(o_a), tok(o_b)], [w_out[:RET_W], w_out[RET_W:]], tl_proj)
    x3 = _moe_layer(x3, 0, modv, p, final=False)

    wq, wk, wv = jnp.split(p['w_qkv_c'][0].astype(BF16), 3, axis=1)
    q, k, v = _norm_proj(x3, p['norm_mix'][1], modv(1, 0), modv(1, 1), [wq, wk, wv],
                         [p['b_q_c'][0][None, :], p['b_k_c'][0][None, :], jnp.zeros((1, SB_W), F32)], tl_proj)
    q, k, v = untok(q), untok(k), untok(v)
    if cache is None:
        o = _sb_prompt(q, k, v, min(512, seq), min(256, seq))
    else:
        cache_k, cache_v, page_table = cache
        o = _sb_sample(q, k, v, cache_k, cache_v, page_table, 8)
    x3 = _out_proj(x3, modv(1, 2), [tok(o)], [p['w_out_c'][0].astype(BF16)], tl_proj)
    y3 = _moe_layer(x3, 1, modv, p, final=True)

    kv_shape = (1, bsz, seq, SB_HEADS, SB_DH)
    return (untok(y3), _unpair(ret_new)[None], _unpair(rwkv_new)[None], pb[:, -1][None],
            k.reshape(kv_shape), v.reshape(kv_shape))


def _moe_layer(x3, l, modv, p, final):
    gsz, lg, _ = x3.shape
    n_tiles = gsz * lg // MOE_TB
    pmax = gsz * lg * TOP_K + n_tiles * N_EXPERTS * (MOE_UNIT - 1) + N_EXPERTS * (MOE_RT - MOE_UNIT)
    pmax = -(-pmax // MOE_RT) * MOE_RT
    h3, sel, cnt = _router(x3, p['norm_ffn'][l], modv(l, 3), modv(l, 4), p['router_w'][l], p['router_bias'][l])
    plan = _moe_plan(cnt[:, :, 0], pmax // MOE_RT)
    xs = _moe_dispatch(h3, sel, plan, pmax)
    ys = _moe_group(xs, plan, p['exp_w_gate'], p['exp_w_up'], p['exp_w_down'], l)
    return _moe_combine(x3, modv(l, 5), h3, sel, ys, plan, p['sh_w_gate'][l].astype(BF16),
                        p['sh_w_up'][l].astype(BF16), p['sh_w_down'][l].astype(BF16), p['final_norm'], final)


def kernel(x_prompt, x_sample, c_prompt, c_sample, state_ret, state_rwkv, state_shift, cache_k, cache_v, page_table, ada_w, ada_b, norm_mix, norm_ffn, final_norm, w_in_ab, w_out_ab, ret_gn_w, ret_gn_b, rwkv_mu, rwkv_w0, rwkv_w_up, rwkv_a0, rwkv_a_up, rwkv_g_up, rwkv_k_k, rwkv_k_a, rwkv_r_k, rwkv_ln_w, rwkv_ln_b, w_qkv_c, b_q_c, b_k_c, w_out_c, router_w, router_bias, exp_w_gate, exp_w_up, exp_w_down, sh_w_gate, sh_w_up, sh_w_down):
    p = dict(ada_w=ada_w, ada_b=ada_b, norm_mix=norm_mix, norm_ffn=norm_ffn, final_norm=final_norm,
             w_in_ab=w_in_ab, w_out_ab=w_out_ab, ret_gn_w=ret_gn_w, ret_gn_b=ret_gn_b, rwkv_mu=rwkv_mu,
             rwkv_w0=rwkv_w0, rwkv_w_up=rwkv_w_up, rwkv_a0=rwkv_a0, rwkv_a_up=rwkv_a_up, rwkv_g_up=rwkv_g_up,
             rwkv_k_k=rwkv_k_k, rwkv_k_a=rwkv_k_a, rwkv_r_k=rwkv_r_k, rwkv_ln_w=rwkv_ln_w, rwkv_ln_b=rwkv_ln_b,
             w_qkv_c=w_qkv_c, b_q_c=b_q_c, b_k_c=b_k_c, w_out_c=w_out_c, router_w=router_w,
             router_bias=router_bias, exp_w_gate=exp_w_gate, exp_w_up=exp_w_up, exp_w_down=exp_w_down,
             sh_w_gate=sh_w_gate, sh_w_up=sh_w_up, sh_w_down=sh_w_down)
    bp, lp, _ = x_prompt.shape
    bs, ls, _ = x_sample.shape
    zeros = lambda *s: jnp.zeros(s, F32)
    y_p, ret_p, rwkv_p, shift_p, k_p, v_p = _trunk(
        x_prompt, c_prompt, jnp.arange(lp), zeros(bp, RET_HEADS, RET_DK, RET_DK),
        zeros(bp, RWKV_HEADS, RWKV_N, RWKV_N), zeros(bp, B_COLS), None, p, flat=False)
    n_pages = page_table.shape[1]
    n_pool = cache_k.shape[1]
    pages = lambda t: jnp.transpose(t, (0, 1, 3, 4, 2)).reshape(1, n_pool, SB_W, PAGE)
    cache = (pages(cache_k), pages(cache_v), page_table)
    y_s, ret_s, rwkv_s, shift_s, k_s, v_s = _trunk(
        x_sample, c_sample, n_pages * PAGE + jnp.arange(ls), state_ret[0], state_rwkv[0], state_shift[0],
        cache, p, flat=True)
    return (y_p, y_s, ret_p, ret_s, rwkv_p, rwkv_s, shift_p, shift_s, k_p, v_p, k_s, v_s)
```

```python
import functools
import math

import jax
import jax.numpy as jnp
from jax import lax
from jax.experimental import pallas as pl
from jax.experimental.pallas import tpu as pltpu

F32 = jnp.float32
BF16 = jnp.bfloat16
HIGHEST = lax.Precision.HIGHEST

D_MODEL = 1024
DEPTH = 2
PAGE = 128
RET_HEADS = 8
RET_DK = 64
RWKV_HEADS = 8
RWKV_N = 64
RWKV_W = RWKV_HEADS * RWKV_N
RET_W = RET_HEADS * RET_DK
LOWRANK = 256
B_COLS = 3 * RWKV_W + LOWRANK
SB_HEADS = 16
SB_DH = 64
SB_W = SB_HEADS * SB_DH
SB_SCALE = SB_DH ** -0.5 * math.log2(math.e)
N_EXPERTS = 64
N_GROUPS = 8
GROUP_SIZE = N_EXPERTS // N_GROUPS
TOPK_GROUPS = 4
TOP_K = 8
HIDDEN = 256
ROUTED_SCALE = 2.5
EPS = 1e-6
ROPE_BASE = 10000.0
LANES = 128
VMEM_LIMIT = 56 * 1024 * 1024

MOE_TB = 256
MOE_UNIT_BITS = 4
MOE_UNIT = 1 << MOE_UNIT_BITS
MOE_RT = 512
MOE_RT_SMALL = 64
MOE_CH = 512
MOE_RMAX = -(-(MOE_TB * TOP_K + N_EXPERTS * (MOE_UNIT - 1)) // MOE_CH) * MOE_CH
MOE_NU = MOE_RMAX // MOE_UNIT
MOE_ISSUE_BITS = 3
MOE_ISSUE = 1 << MOE_ISSUE_BITS
MOE_INFO_BITS = 9
MOE_INFO_SHIFT = 1 << MOE_INFO_BITS

RWKV_CHUNK = 16
RWKV_NSEQ = 4
RWKV_HP_LOCAL = False
RWKV_HP_STATE = False


def _cparams(sem):
    return pltpu.CompilerParams(dimension_semantics=sem, vmem_limit_bytes=VMEM_LIMIT)


def _dot(a, b, hp=False):
    if hp:
        return jnp.dot(a.astype(F32), b.astype(F32), preferred_element_type=F32, precision=HIGHEST)
    return jnp.dot(a.astype(BF16), b.astype(BF16), preferred_element_type=F32)


def _dot_nt(a, b, hp=False):
    dn = (((1,), (1,)), ((), ()))
    if hp:
        return lax.dot_general(a.astype(F32), b.astype(F32), dn, preferred_element_type=F32, precision=HIGHEST)
    return lax.dot_general(a.astype(BF16), b.astype(BF16), dn, preferred_element_type=F32)


def _dot_tn(a, b, hp=False):
    dn = (((0,), (0,)), ((), ()))
    if hp:
        return lax.dot_general(a.astype(F32), b.astype(F32), dn, preferred_element_type=F32, precision=HIGHEST)
    return lax.dot_general(a.astype(BF16), b.astype(BF16), dn, preferred_element_type=F32)


def _split_bf16(x):
    hi = x.astype(BF16)
    return hi, x - hi.astype(F32)


def _dot_x3(a, b):
    ah, al = _split_bf16(a)
    bh, bl = _split_bf16(b)
    d = lambda x, y: jnp.dot(x, y.astype(BF16), preferred_element_type=F32)
    return d(ah, bh) + (d(ah, bl) + d(al.astype(BF16), bh))


def _dot_exact_lhs(a, b):
    a = a.astype(BF16)
    b1, r1 = _split_bf16(b)
    b2, r2 = _split_bf16(r1)
    d = lambda y: jnp.dot(a, y.astype(BF16), preferred_element_type=F32)
    return d(b1) + (d(b2) + d(r2))


def _silu(x):
    return x * jax.nn.sigmoid(x)


def _lo_mask(shape):
    return (lax.broadcasted_iota(jnp.int32, shape, len(shape) - 1) % LANES) < (LANES // 2)


def _pair_sum(x, lo):
    s_lo = jnp.sum(jnp.where(lo, x, 0.0), axis=-1, keepdims=True)
    s_hi = jnp.sum(jnp.where(lo, 0.0, x), axis=-1, keepdims=True)
    return jnp.where(lo, s_lo, s_hi)


def _ada_kernel(c_ref, w_ref, b_ref, o_ref):
    o_ref[0] = _dot(_silu(c_ref[...]), w_ref[0]) + b_ref[0]


def _ada(c, ada_w, ada_b):
    bsz = c.shape[0]
    n = ada_w.shape[-1]
    tn = D_MODEL
    return pl.pallas_call(
        _ada_kernel,
        grid=(DEPTH, n // tn),
        in_specs=[
            pl.BlockSpec((bsz, D_MODEL), lambda l, j: (0, 0)),
            pl.BlockSpec((1, D_MODEL, tn), lambda l, j: (l, 0, j)),
            pl.BlockSpec((1, 1, tn), lambda l, j: (l, 0, j)),
        ],
        out_specs=pl.BlockSpec((1, bsz, tn), lambda l, j: (l, 0, j)),
        out_shape=jax.ShapeDtypeStruct((DEPTH, bsz, n), F32),
        compiler_params=_cparams(("parallel", "parallel")),
        name="ada_mod",
    )(c, ada_w, ada_b.reshape(DEPTH, 1, n))


def _modulated_norm(x, g, sh, sc):
    y = x * lax.rsqrt(jnp.mean(x * x, axis=-1, keepdims=True) + EPS)
    return (y * g) * (1.0 + sc) + sh


def _norm_proj_kernel(n_out, x_ref, g_ref, sh_ref, sc_ref, *refs):
    w_refs, b_refs, o_refs = refs[:n_out], refs[n_out:2 * n_out], refs[2 * n_out:]
    h = _modulated_norm(x_ref[0], g_ref[...], sh_ref[0], sc_ref[0]).astype(BF16)
    for w_ref, b_ref, o_ref in zip(w_refs, b_refs, o_refs):
        o_ref[0] = jnp.dot(h, w_ref[...], preferred_element_type=F32) + b_ref[...]


def _norm_proj(x3, g, sh3, sc3, ws, bs, tl):
    gsz, lg, _ = x3.shape
    lm = sh3.shape[1]
    tm = 1 if lm == 1 else tl
    mod_map = (lambda b, i: (b, 0, 0)) if lm == 1 else (lambda b, i: (b, i, 0))
    n_out = len(ws)
    in_specs = [
        pl.BlockSpec((1, tl, D_MODEL), lambda b, i: (b, i, 0)),
        pl.BlockSpec((1, D_MODEL), lambda b, i: (0, 0)),
        pl.BlockSpec((1, tm, D_MODEL), mod_map),
        pl.BlockSpec((1, tm, D_MODEL), mod_map),
    ]
    in_specs += [pl.BlockSpec(w.shape, lambda b, i: (0, 0)) for w in ws]
    in_specs += [pl.BlockSpec(b.shape, lambda b, i: (0, 0)) for b in bs]
    return pl.pallas_call(
        functools.partial(_norm_proj_kernel, n_out),
        grid=(gsz, lg // tl),
        in_specs=in_specs,
        out_specs=[pl.BlockSpec((1, tl, w.shape[1]), lambda b, i: (b, i, 0)) for w in ws],
        out_shape=[jax.ShapeDtypeStruct((gsz, lg, w.shape[1]), F32) for w in ws],
        compiler_params=_cparams(("parallel", "parallel")),
        name="norm_proj",
    )(x3, g.reshape(1, D_MODEL), sh3, sc3, *ws, *bs)


def _out_proj_kernel(n_in, x_ref, gate_ref, *refs):
    o_refs, w_refs, y_ref = refs[:n_in], refs[n_in:2 * n_in], refs[2 * n_in]
    acc = None
    for o_ref, w_ref in zip(o_refs, w_refs):
        t = jnp.dot(o_ref[0].astype(BF16), w_ref[...], preferred_element_type=F32)
        acc = t if acc is None else acc + t
    y_ref[0] = x_ref[0] + gate_ref[0] * acc


def _out_proj(x3, gate3, os_, ws, tl):
    gsz, lg, _ = x3.shape
    lm = gate3.shape[1]
    tm = 1 if lm == 1 else tl
    mod_map = (lambda b, i: (b, 0, 0)) if lm == 1 else (lambda b, i: (b, i, 0))
    n_in = len(os_)
    in_specs = [
        pl.BlockSpec((1, tl, D_MODEL), lambda b, i: (b, i, 0)),
        pl.BlockSpec((1, tm, D_MODEL), mod_map),
    ]
    in_specs += [pl.BlockSpec((1, tl, o.shape[-1]), lambda b, i: (b, i, 0)) for o in os_]
    in_specs += [pl.BlockSpec(w.shape, lambda b, i: (0, 0)) for w in ws]
    return pl.pallas_call(
        functools.partial(_out_proj_kernel, n_in),
        grid=(gsz, lg // tl),
        in_specs=in_specs,
        out_specs=pl.BlockSpec((1, tl, D_MODEL), lambda b, i: (b, i, 0)),
        out_shape=jax.ShapeDtypeStruct(x3.shape, F32),
        compiler_params=_cparams(("parallel", "parallel")),
        name="out_proj",
    )(x3, gate3, *os_, *ws)


def _rot_half(x, lo32):
    return jnp.where(lo32, pltpu.roll(x, LANES - 32, 1), pltpu.roll(x, 32, 1))


def _retention_kernel(chunk, lgs_ref, q_ref, k_ref, v_ref, g_ref, cos_ref, sin_ref, gw_ref, gb_ref, s0_ref,
                      o_ref, s_ref, st_ref):
    npair = RET_HEADS // 2
    c = pl.program_id(1)

    @pl.when(c == 0)
    def _():
        st_ref[...] = s0_ref[0]

    lane = lax.broadcasted_iota(jnp.int32, (chunk, LANES), 1)
    lo = lane < (LANES // 2)
    lo32 = (lane % RET_DK) < (RET_DK // 2)
    cos = cos_ref[...]
    sin = sin_ref[...]
    ti = lax.broadcasted_iota(jnp.int32, (chunk, LANES), 0).astype(F32)
    ii = lax.broadcasted_iota(jnp.int32, (chunk, chunk), 0)
    jj = lax.broadcasted_iota(jnp.int32, (chunk, chunk), 1)
    diff = jnp.maximum(ii - jj, 0).astype(F32)
    causal = ii >= jj
    ri = lax.broadcasted_iota(jnp.int32, (LANES, LANES), 0)
    ci = lax.broadcasted_iota(jnp.int32, (LANES, LANES), 1)
    same_head = (ri < RET_DK) == (ci < RET_DK)

    def pair(p, st):
        sl = slice(p * LANES, (p + 1) * LANES)
        lg_a = lgs_ref[2 * p]
        lg_b = lgs_ref[2 * p + 1]
        lg = jnp.where(lo, lg_a, lg_b)
        q = q_ref[0, :, sl]
        k = k_ref[0, :, sl]
        q = q * cos + _rot_half(q, lo32) * sin
        k = (k * cos + _rot_half(k, lo32) * sin) * (RET_DK ** -0.5)
        kb = k.astype(BF16)
        vb = v_ref[0, :, sl].astype(BF16)

        def head(lg_h, sel):
            dmask = jnp.where(causal, jnp.exp(lg_h * diff), 0.0)
            sc = _dot_nt(jnp.where(sel, q, 0.0), kb) * dmask
            return _dot(sc, vb)

        o = jnp.where(lo, head(lg_a, lo), head(lg_b, jnp.logical_not(lo)))
        o = o + _dot(q * jnp.exp(lg * (ti + 1.0)), st)
        c_dec = jnp.exp(jnp.where(ri < RET_DK, lg_a, lg_b) * float(chunk))
        st_new = st * c_dec + jnp.where(same_head, _dot_tn(k * jnp.exp(lg * (chunk - 1.0 - ti)), vb), 0.0)
        mu = _pair_sum(o, lo) * (1.0 / RET_DK)
        d = o - mu
        var = _pair_sum(d * d, lo) * (1.0 / RET_DK)
        y = (d * lax.rsqrt(var + 1e-5)) * gw_ref[:, sl] + gb_ref[:, sl]
        return y * _silu(g_ref[0, :, sl]), st_new

    states = [st_ref[p] for p in range(npair)]
    results = [pair(p, states[p]) for p in range(npair)]
    for p, (y, st_new) in enumerate(results):
        o_ref[0, :, p * LANES:(p + 1) * LANES] = y
        st_ref[p] = st_new

    @pl.when(c == pl.num_programs(1) - 1)
    def _():
        for p, (_, st_new) in enumerate(results):
            s_ref[0, p] = st_new


def _retention(pa, cos_t, sin_t, lgs, gn_w, gn_b, s0_bd, chunk):
    bsz, seq, _ = pa.shape
    npair = RET_HEADS // 2
    nc = seq // chunk
    col = lambda j: pl.BlockSpec((1, chunk, RET_W), lambda b, c: (b, c, j))
    return pl.pallas_call(
        functools.partial(_retention_kernel, chunk),
        grid=(bsz, nc),
        in_specs=[
            pl.BlockSpec(memory_space=pltpu.SMEM),
            col(0), col(1), col(2), col(3),
            pl.BlockSpec((chunk, LANES), lambda b, c: (c, 0)),
            pl.BlockSpec((chunk, LANES), lambda b, c: (c, 0)),
            pl.BlockSpec((1, RET_W), lambda b, c: (0, 0)),
            pl.BlockSpec((1, RET_W), lambda b, c: (0, 0)),
            pl.BlockSpec((1, npair, LANES, LANES), lambda b, c: (b, 0, 0, 0)),
        ],
        out_specs=[
            pl.BlockSpec((1, chunk, RET_W), lambda b, c: (b, c, 0)),
            pl.BlockSpec((1, npair, LANES, LANES), lambda b, c: (b, 0, 0, 0)),
        ],
        out_shape=[
            jax.ShapeDtypeStruct((bsz, seq, RET_W), F32),
            jax.ShapeDtypeStruct((bsz, npair, LANES, LANES), F32),
        ],
        scratch_shapes=[pltpu.VMEM((npair, LANES, LANES), F32)],
        compiler_params=_cparams(("parallel", "arbitrary")),
        name="retention",
    )(lgs, pa, pa, pa, pa, cos_t, sin_t, gn_w.reshape(1, RET_W), gn_b.reshape(1, RET_W), s0_bd)


def _rwkv_kernel(nseq, tb, chunk, pb_ref, prev_ref, s0_ref, mu_ref, wlr_ref, w0_ref, a0_ref, kk_ref, ka_ref, rk_ref,
                 lnw_ref, lnb_ref, o_ref, s_ref, *scratch):
    npair = RWKV_HEADS // 2
    nchain = nseq * npair
    st_refs = scratch[:nchain]
    carry_ref, a_s, b_s, k_s, r_s, v_s, bp_s, kp_s, pc_s, y_s, tinv_s, lrb_s, lva_s, lvr_s = scratch[nchain:]
    blk = pl.program_id(1)

    @pl.when(blk == 0)
    def _():
        for s in range(nseq):
            for p in range(npair):
                st_refs[s * npair + p][...] = s0_ref[s, p]
            carry_ref[s] = prev_ref[s]

    lo_full = _lo_mask((tb, LANES))
    ti = lax.broadcasted_iota(jnp.int32, (tb, tb), 0)
    tj = lax.broadcasted_iota(jnp.int32, (tb, tb), 1)
    same_chunk = (ti // chunk) == (tj // chunk)
    sel = jnp.concatenate([jnp.logical_and(same_chunk, ti >= tj), same_chunk], axis=0).astype(F32)

    def prepare(s):
        rs = slice(s * tb, (s + 1) * tb)
        pb = pb_ref[s]
        row = lax.broadcasted_iota(jnp.int32, pb.shape, 0)
        prev = jnp.where(row == 0, carry_ref[s], pltpu.roll(pb, 1, 0))
        carry_ref[s] = pb[tb - 1:tb, :]
        pbs = pb + (prev - pb) * mu_ref[...]
        r = pbs[:, :RWKV_W]
        kb = pbs[:, RWKV_W:2 * RWKV_W]
        v = pbs[:, 2 * RWKV_W:3 * RWKV_W]
        tail = pbs[:, 3 * RWKV_W:]
        tl_lane = lax.broadcasted_iota(jnp.int32, tail.shape, 1)
        act = jnp.where(tl_lane < 64, jnp.tanh(tail), jnp.where(tl_lane < 128, tail, jax.nn.sigmoid(tail)))
        lr = _dot(act, wlr_ref[...])
        wz = -(w0_ref[...] + lr[:, :RWKV_W])
        softplus = jnp.maximum(wz, 0.0) + jnp.log1p(jnp.exp(-jnp.abs(wz)))
        logw = -jnp.exp(-softplus - 0.5)
        a = jax.nn.sigmoid(a0_ref[...] + lr[:, RWKV_W:2 * RWKV_W])
        kkf = kb * kk_ref[...]
        kb2 = kb * (1.0 + (a - 1.0) * ka_ref[...])
        bonus_src = r * kb2 * rk_ref[...]
        cums = _dot_exact_lhs(sel, logw)
        cum, tot = cums[:tb], cums[tb:]
        pin = jnp.exp(cum)
        pinv = jnp.exp(-cum)
        prem = jnp.exp(tot - cum)
        bonus = []
        for p in range(npair):
            sl = slice(p * LANES, (p + 1) * LANES)
            kf = kkf[:, sl]
            kn = kf * lax.rsqrt(jnp.maximum(_pair_sum(kf * kf, lo_full), 1e-24))
            bv = kn * a[:, sl]
            a_s[rs, sl] = -kn * jnp.exp(cum[:, sl] - logw[:, sl])
            b_s[rs, sl] = bv * pinv[:, sl]
            bp_s[rs, sl] = bv * prem[:, sl]
            bonus.append(_pair_sum(bonus_src[:, sl], lo_full) * v[:, sl])
        k_s[rs, :] = kb2 * pinv
        kp_s[rs, :] = kb2 * prem
        r_s[rs, :] = r * pin
        v_s[rs, :] = v
        pc_s[rs, :] = jnp.exp(tot)
        return bonus, lr[:, 2 * RWKV_W:]

    prepared = [prepare(s) for s in range(nseq)]

    c2 = 2 * chunk
    lo = _lo_mask((chunk, LANES))
    r2 = lax.broadcasted_iota(jnp.int32, (c2, c2), 0)
    q2 = lax.broadcasted_iota(jnp.int32, (c2, c2), 1)
    same = (r2 // chunk) == (q2 // chunk)
    strict = jnp.logical_and(same, (r2 % chunk) > (q2 % chunk))
    incl = jnp.logical_and(same, (r2 % chunk) >= (q2 % chunk))
    eye = (r2 == q2).astype(F32)
    n_sq = int(math.log2(chunk)) - 1
    pairs = range(nchain)
    lanes = [slice((c % npair) * LANES, (c % npair + 1) * LANES) for c in pairs]
    row0 = [(c // npair) * tb for c in pairs]

    def stack(x):
        return jnp.concatenate([jnp.where(lo, x, 0.0), jnp.where(lo, 0.0, x)], axis=0)

    def chunk_rows(ci):
        return [pl.ds(pl.multiple_of(row0[c] + ci * chunk, chunk), chunk) for c in pairs]

    def local_step(ci, carry):
        rows = chunk_rows(ci)
        ar = [jnp.concatenate([stack(a_s[rows[p], lanes[p]]), stack(r_s[rows[p], lanes[p]])], axis=0)
              for p in pairs]
        b2 = [b_s[rows[p], lanes[p]] for p in pairs]
        k2 = [k_s[rows[p], lanes[p]] for p in pairs]
        gb_ = [_dot_nt(ar[p], jnp.concatenate([b2[p], b2[p]], axis=0), hp=RWKV_HP_LOCAL) for p in pairs]
        gk_ = [_dot_nt(ar[p], jnp.concatenate([k2[p], k2[p]], axis=0), hp=RWKV_HP_LOCAL) for p in pairs]
        l_ab = [jnp.where(strict, gb_[p][:c2], 0.0) for p in pairs]
        tinv = [eye + l_ab[p] for p in pairs]
        xp = [_dot_x3(l_ab[p], l_ab[p]) for p in pairs]
        for _ in range(n_sq - 1):
            both = [_dot_x3(jnp.concatenate([tinv[p], xp[p]], axis=0), xp[p]) for p in pairs]
            tinv = [tinv[p] + both[p][:c2] for p in pairs]
            xp = [both[p][c2:] for p in pairs]
        tinv = [tinv[p] + _dot_x3(tinv[p], xp[p]) for p in pairs]
        for p in pairs:
            l_akrk = jnp.concatenate([jnp.where(strict, gk_[p][:c2], 0.0), jnp.where(incl, gk_[p][c2:], 0.0)], axis=0)
            lv = _dot(l_akrk, stack(v_s[rows[p], lanes[p]]), hp=RWKV_HP_LOCAL)
            tinv_s[ci, p] = tinv[p]
            lrb_s[ci, p] = jnp.where(incl, gb_[p][c2:], 0.0)
            lva_s[ci, p] = lv[:c2]
            lvr_s[ci, p] = lv[c2:]
        return carry

    lax.fori_loop(0, tb // chunk, local_step, 0)

    def state_step(ci, carry):
        rows = chunk_rows(ci)
        st = [st_refs[p][...] for p in pairs]
        ar = [jnp.concatenate([stack(a_s[rows[p], lanes[p]]), stack(r_s[rows[p], lanes[p]])], axis=0) for p in pairs]
        ars = [_dot_nt(ar[p], st[p], hp=RWKV_HP_STATE) for p in pairs]
        u_st = [_dot(tinv_s[ci, p], ars[p][:c2] + lva_s[ci, p], hp=RWKV_HP_STATE) for p in pairs]
        y_st = [ars[p][c2:] + lvr_s[ci, p] + _dot(lrb_s[ci, p], u_st[p], hp=RWKV_HP_STATE) for p in pairs]
        new = []
        for p in pairs:
            r, s = rows[p], lanes[p]
            uv = jnp.concatenate([u_st[p], stack(v_s[r, s])], axis=0)
            bk = jnp.concatenate([stack(bp_s[r, s]), stack(kp_s[r, s])], axis=0)
            pc = pc_s[pl.ds(row0[p] + ci * chunk, 1), s]
            new.append(st[p] * pc + _dot_tn(uv, bk, hp=RWKV_HP_STATE))
        for p in pairs:
            y_s[rows[p], lanes[p]] = y_st[p][:chunk] + y_st[p][chunk:]
            st_refs[p][...] = new[p]
        return carry

    lax.fori_loop(0, tb // chunk, state_step, 0)

    for s in range(nseq):
        bonus, gate = prepared[s]
        for p in range(npair):
            sl = slice(p * LANES, (p + 1) * LANES)
            y = y_s[s * tb:(s + 1) * tb, sl]
            mu = _pair_sum(y, lo_full) * (1.0 / RWKV_N)
            d = y - mu
            var = _pair_sum(d * d, lo_full) * (1.0 / RWKV_N)
            yn = (d * lax.rsqrt(var + 64e-5)) * lnw_ref[:, sl] + lnb_ref[:, sl]
            o_ref[s, :, sl] = (yn + bonus[p]) * gate[:, sl]

    @pl.when(blk == pl.num_programs(1) - 1)
    def _():
        for s in range(nseq):
            for p in range(npair):
                s_ref[s, p] = st_refs[s * npair + p][...]


def _rwkv(pb, prev, s0_bd, mu, wlr, w0, a0, k_k, k_a, r_k, ln_w, ln_b, tb, chunk):
    bsz, seq, _ = pb.shape
    npair = RWKV_HEADS // 2
    vec = lambda n: pl.BlockSpec((1, n), lambda b, i: (0, 0))
    nseq = RWKV_NSEQ
    nchain = nseq * npair
    scr = lambda: pltpu.VMEM((nseq * tb, RWKV_W), F32)
    nch, c2 = tb // chunk, 2 * chunk
    return pl.pallas_call(
        functools.partial(_rwkv_kernel, nseq, tb, chunk),
        grid=(bsz // nseq, seq // tb),
        in_specs=[
            pl.BlockSpec((nseq, tb, B_COLS), lambda b, i: (b, i, 0)),
            pl.BlockSpec((nseq, 1, B_COLS), lambda b, i: (b, 0, 0)),
            pl.BlockSpec((nseq, npair, LANES, LANES), lambda b, i: (b, 0, 0, 0)),
            vec(B_COLS),
            pl.BlockSpec(wlr.shape, lambda b, i: (0, 0)),
            vec(RWKV_W), vec(RWKV_W), vec(RWKV_W), vec(RWKV_W), vec(RWKV_W), vec(RWKV_W), vec(RWKV_W),
        ],
        out_specs=[
            pl.BlockSpec((nseq, tb, RWKV_W), lambda b, i: (b, i, 0)),
            pl.BlockSpec((nseq, npair, LANES, LANES), lambda b, i: (b, 0, 0, 0)),
        ],
        out_shape=[
            jax.ShapeDtypeStruct((bsz, seq, RWKV_W), F32),
            jax.ShapeDtypeStruct((bsz, npair, LANES, LANES), F32),
        ],
        scratch_shapes=[pltpu.VMEM((LANES, LANES), F32) for _ in range(nchain)] + [
            pltpu.VMEM((nseq, 1, B_COLS), F32),
            scr(), scr(), scr(), scr(), scr(), scr(), scr(), scr(), scr(),
            pltpu.VMEM((nch, nchain, c2, c2), F32), pltpu.VMEM((nch, nchain, c2, c2), F32),
            pltpu.VMEM((nch, nchain, c2, LANES), F32), pltpu.VMEM((nch, nchain, c2, LANES), F32),
        ],
        compiler_params=_cparams(("parallel", "arbitrary")),
        name="rwkv7",
    )(pb, prev, s0_bd, mu.reshape(1, -1), wlr, w0.reshape(1, -1), a0.reshape(1, -1), k_k.reshape(1, -1),
      k_a.reshape(1, -1), r_k.reshape(1, -1), ln_w.reshape(1, -1), ln_b.reshape(1, -1))


def _sb_block(z, mask, carry, tri_u):
    log_w, rem_sum = _sb_block_logs(z, mask, tri_u)
    return _sb_block_weights(log_w, carry, mask), carry + rem_sum


def _sb_block_logs(z, mask, tri_u):
    log_beta = jnp.minimum(z, 0.0) - jnp.log2(1.0 + jnp.exp2(-jnp.abs(z)))
    log_rem = log_beta - z
    if mask is not None:
        log_rem = jnp.where(mask, log_rem, 0.0)
    hi = log_rem.astype(BF16)
    lo = (log_rem - hi.astype(F32)).astype(BF16)
    after = jnp.dot(jnp.concatenate([hi, lo], axis=1), tri_u, preferred_element_type=F32)
    return log_beta + after, jnp.sum(log_rem, axis=-1, keepdims=True)


def _sb_block_weights(log_w, carry, mask):
    w = jnp.exp2(log_w + carry)
    return w if mask is None else jnp.where(mask, w, 0.0)


def _strict_upper(n):
    r = lax.broadcasted_iota(jnp.int32, (2 * n, n), 0) % n
    c = lax.broadcasted_iota(jnp.int32, (2 * n, n), 1)
    return (r > c).astype(BF16)


def _sb_prompt_kernel(tq, tk, q_ref, k_ref, v_ref, o_ref):
    qi = pl.program_id(2)
    q = q_ref[0] * SB_SCALE
    lo = _lo_mask((tq, LANES))
    qa = jnp.where(lo, q, 0.0).astype(BF16)
    qb = jnp.where(lo, 0.0, q).astype(BF16)
    tri_u = _strict_upper(tk)
    qpos = qi * tq + lax.broadcasted_iota(jnp.int32, (tq, tk), 0)
    kofs = lax.broadcasted_iota(jnp.int32, (tq, tk), 1)
    n_diag = tq // tk
    n_full = qi * n_diag

    def visit(kb_idx, state, masked):
        acc_a, acc_b, car_a, car_b = state
        rows = pl.ds(pl.multiple_of(kb_idx * tk, tk), tk)
        kblk = k_ref[0, rows, :].astype(BF16)
        vblk = v_ref[0, rows, :].astype(BF16)
        mask = ((kb_idx * tk + kofs) < qpos) if masked else None
        w_a, car_a = _sb_block(_dot_nt(qa, kblk), mask, car_a, tri_u)
        w_b, car_b = _sb_block(_dot_nt(qb, kblk), mask, car_b, tri_u)
        acc_a = acc_a + jnp.dot(w_a.astype(BF16), vblk, preferred_element_type=F32)
        acc_b = acc_b + jnp.dot(w_b.astype(BF16), vblk, preferred_element_type=F32)
        return acc_a, acc_b, car_a, car_b

    zero = jnp.zeros((tq, LANES), F32)
    zc = jnp.zeros((tq, 1), F32)
    state = (zero, zero, zc, zc)
    for d in reversed(range(n_diag)):
        state = visit(n_full + d, state, True)
    state = lax.fori_loop(0, n_full, lambda j, s: visit(n_full - 1 - j, s, False), state)
    o_ref[0] = jnp.where(lo, state[0], state[1])


def _sb_prompt(q, k, v, tq, tk):
    bsz, seq, _ = q.shape
    npair = SB_HEADS // 2
    return pl.pallas_call(
        functools.partial(_sb_prompt_kernel, tq, tk),
        grid=(bsz, npair, seq // tq),
        in_specs=[
            pl.BlockSpec((1, tq, LANES), lambda b, h, i: (b, i, h)),
            pl.BlockSpec((1, seq, LANES), lambda b, h, i: (b, 0, h)),
            pl.BlockSpec((1, seq, LANES), lambda b, h, i: (b, 0, h)),
        ],
        out_specs=pl.BlockSpec((1, tq, LANES), lambda b, h, i: (b, i, h)),
        out_shape=jax.ShapeDtypeStruct((bsz, seq, SB_W), F32),
        compiler_params=_cparams(("parallel", "parallel", "arbitrary")),
        name="sb_prompt",
    )(q, k, v)


def _sb_sample_kernel(npg, lq, *refs):
    pt_ref = refs[0]
    q_ref, kn_ref, vn_ref = refs[1:4]
    kp_refs = refs[4:4 + npg]
    vp_refs = refs[4 + npg:4 + 2 * npg]
    o_ref = refs[4 + 2 * npg]
    qbd_ref, acc_ref, car_ref = refs[5 + 2 * npg:]
    del pt_ref
    j = pl.program_id(1)
    rows = SB_HEADS * lq
    tri_u = _strict_upper(PAGE)

    def visit_new(kblk, vblk, mask):
        w, car = _sb_block(_dot_nt(qbd_ref[...], kblk), mask, car_ref[...], tri_u)
        car_ref[...] = car
        acc_ref[...] += _dot(w, vblk)

    @pl.when(j == 0)
    def _():
        q = q_ref[0] * SB_SCALE
        qt = jnp.concatenate([q] * SB_HEADS, axis=0)
        rh = lax.broadcasted_iota(jnp.int32, (rows, SB_W), 0) // lq
        ch = lax.broadcasted_iota(jnp.int32, (rows, SB_W), 1) // SB_DH
        qbd_ref[...] = jnp.where(rh == ch, qt, 0.0).astype(BF16)
        acc_ref[...] = jnp.zeros_like(acc_ref)
        car_ref[...] = jnp.zeros_like(car_ref)
        pad = jnp.zeros((PAGE - lq, SB_W), F32)
        kblk = jnp.concatenate([kn_ref[0], pad], axis=0)
        vblk = jnp.concatenate([vn_ref[0], pad], axis=0)
        qidx = lax.broadcasted_iota(jnp.int32, (rows, PAGE), 0) % lq
        kidx = lax.broadcasted_iota(jnp.int32, (rows, PAGE), 1)
        visit_new(kblk, vblk, kidx < qidx)

    k_all = jnp.concatenate([kp_ref[0, 0].astype(BF16) for kp_ref in kp_refs], axis=1)
    z = jnp.dot(qbd_ref[...], k_all, preferred_element_type=F32)
    log_beta = jnp.minimum(z, 0.0) - jnp.log2(1.0 + jnp.exp2(-jnp.abs(z)))
    log_rem = log_beta - z
    hi = log_rem.astype(BF16)
    lo = (log_rem - hi.astype(F32)).astype(BF16)
    page = lambda t, p: t[:, p * PAGE:(p + 1) * PAGE]
    split = jnp.concatenate([jnp.concatenate([page(hi, p), page(lo, p)], axis=1) for p in range(npg)], axis=0)
    after = jnp.dot(split, tri_u, preferred_element_type=F32)
    car = car_ref[...]
    ws = []
    for p in range(npg):
        ws.append(jnp.exp2(page(log_beta, p) + after[p * rows:(p + 1) * rows] + car).astype(BF16))
        car = car + jnp.sum(page(log_rem, p), axis=-1, keepdims=True)
    car_ref[...] = car
    v_all = jnp.concatenate([vp_ref[0, 0].astype(BF16) for vp_ref in vp_refs], axis=1)
    acc_ref[...] += _dot_nt(jnp.concatenate(ws, axis=1), v_all)

    @pl.when(j == pl.num_programs(1) - 1)
    def _():
        acc = acc_ref[...]
        ch = lax.broadcasted_iota(jnp.int32, (lq, SB_W), 1) // SB_DH
        out = jnp.zeros((lq, SB_W), F32)
        for h in range(SB_HEADS):
            out = out + jnp.where(ch == h, acc[h * lq:(h + 1) * lq, :], 0.0)
        o_ref[0] = out


def _sb_sample(q, k_new, v_new, cache_k, cache_v, page_table, npg):
    bsz, lq, _ = q.shape
    n_pages = page_table.shape[1]
    rows = SB_HEADS * lq
    tok = pl.BlockSpec((1, lq, SB_W), lambda b, j, pt: (b, 0, 0))

    def page_spec(i):
        return pl.BlockSpec((1, 1, SB_W, PAGE), lambda b, j, pt: (0, pt[b, n_pages - 1 - (j * npg + i)], 0, 0))

    grid_spec = pltpu.PrefetchScalarGridSpec(
        num_scalar_prefetch=1,
        grid=(bsz, n_pages // npg),
        in_specs=[tok, tok, tok] + [page_spec(i) for i in range(npg)] * 2,
        out_specs=pl.BlockSpec((1, lq, SB_W), lambda b, j, pt: (b, 0, 0)),
        scratch_shapes=[
            pltpu.VMEM((rows, SB_W), BF16),
            pltpu.VMEM((rows, SB_W), F32),
            pltpu.VMEM((rows, 1), F32),
        ],
    )
    return pl.pallas_call(
        functools.partial(_sb_sample_kernel, npg, lq),
        grid_spec=grid_spec,
        out_shape=jax.ShapeDtypeStruct((bsz, lq, SB_W), F32),
        compiler_params=_cparams(("parallel", "arbitrary")),
        name="sb_sample",
    )(page_table, q, k_new, v_new, *([cache_k] * npg), *([cache_v] * npg))


def _first_index(cond, idx, big):
    return jnp.min(jnp.where(cond, idx, big), axis=(0, 1), keepdims=True)


def _router_kernel(x_ref, g_ref, sh_ref, sc_ref, rw_ref, rb_ref, h_ref, comb_ref, cnt_ref):
    h = _modulated_norm(x_ref[0], g_ref[...], sh_ref[0], sc_ref[0])
    h_ref[0] = h.astype(BF16)
    tl = h.shape[0]
    logits = _dot_nt(rw_ref[...], h, hp=True)
    scores = jax.nn.sigmoid(logits).reshape(N_GROUPS, GROUP_SIZE, tl)
    biased = scores + rb_ref[...].reshape(N_GROUPS, GROUP_SIZE, 1)
    neg = -jnp.inf

    jidx = lax.broadcasted_iota(jnp.int32, biased.shape, 1)
    m1 = jnp.max(biased, axis=1, keepdims=True)
    first = jnp.min(jnp.where(biased == m1, jidx, GROUP_SIZE), axis=1, keepdims=True)
    m2 = jnp.max(jnp.where(jidx == first, neg, biased), axis=1, keepdims=True)
    gscore = m1 + m2

    gidx = lax.broadcasted_iota(jnp.int32, gscore.shape, 0)
    gsel = jnp.zeros(gscore.shape, jnp.bool_)
    for _ in range(TOPK_GROUPS):
        m = jnp.max(gscore, axis=0, keepdims=True)
        pick = gidx == jnp.min(jnp.where(gscore == m, gidx, N_GROUPS), axis=0, keepdims=True)
        gsel = jnp.logical_or(gsel, pick)
        gscore = jnp.where(pick, neg, gscore)

    cand = jnp.where(gsel, biased, neg)
    eidx = lax.broadcasted_iota(jnp.int32, cand.shape, 0) * GROUP_SIZE + jidx
    esel = jnp.zeros(cand.shape, jnp.bool_)
    for _ in range(TOP_K):
        m = jnp.max(cand, axis=(0, 1), keepdims=True)
        pick = eidx == _first_index(cand == m, eidx, N_EXPERTS)
        esel = jnp.logical_or(esel, pick)
        cand = jnp.where(pick, neg, cand)

    sel = jnp.where(esel, scores, 0.0)
    wts = (sel / jnp.sum(sel, axis=(0, 1), keepdims=True) * ROUTED_SCALE).reshape(N_EXPERTS, tl)
    comb_ref[0] = wts
    cnt = jnp.sum((wts > 0.0).astype(jnp.int32), axis=-1, keepdims=True)
    cnt_ref[0] = jnp.broadcast_to(cnt, (N_EXPERTS, LANES))


def _router(x3, g, sh3, sc3, router_w, router_bias):
    gsz, lg, _ = x3.shape
    tl = MOE_TB
    tpg = lg // tl
    lm = sh3.shape[1]
    tm = 1 if lm == 1 else tl
    mod_map = (lambda b, i: (b, 0, 0)) if lm == 1 else (lambda b, i: (b, i, 0))
    return pl.pallas_call(
        _router_kernel,
        grid=(gsz, tpg),
        in_specs=[
            pl.BlockSpec((1, tl, D_MODEL), lambda b, i: (b, i, 0)),
            pl.BlockSpec((1, D_MODEL), lambda b, i: (0, 0)),
            pl.BlockSpec((1, tm, D_MODEL), mod_map),
            pl.BlockSpec((1, tm, D_MODEL), mod_map),
            pl.BlockSpec((N_EXPERTS, D_MODEL), lambda b, i: (0, 0)),
            pl.BlockSpec((N_EXPERTS, 1), lambda b, i: (0, 0)),
        ],
        out_specs=[
            pl.BlockSpec((1, tl, D_MODEL), lambda b, i: (b, i, 0)),
            pl.BlockSpec((1, N_EXPERTS, tl), lambda b, i: (b * tpg + i, 0, 0)),
            pl.BlockSpec((1, N_EXPERTS, LANES), lambda b, i: (b * tpg + i, 0, 0)),
        ],
        out_shape=[
            jax.ShapeDtypeStruct((gsz, lg, D_MODEL), BF16),
            jax.ShapeDtypeStruct((gsz * tpg, N_EXPERTS, tl), F32),
            jax.ShapeDtypeStruct((gsz * tpg, N_EXPERTS, LANES), jnp.int32),
        ],
        compiler_params=_cparams(("parallel", "parallel")),
        name="moe_router",
    )(x3, g.reshape(1, D_MODEL), sh3, sc3, router_w.T, router_bias.reshape(N_EXPERTS, 1))


def _swiglu_hidden(h, wg, wu):
    return _silu(jnp.dot(h, wg, preferred_element_type=F32)) * jnp.dot(h, wu, preferred_element_type=F32)


def _moe_plan(cnt, nt_max, rt):
    e_ids = N_EXPERTS
    pc = (cnt + MOE_UNIT - 1) // MOE_UNIT * MOE_UNIT
    lend = jnp.cumsum(pc, axis=1)
    lstart = lend - pc
    nun = (lend[:, -1] // MOE_UNIT).astype(jnp.int32)
    ctot = jnp.sum(pc, axis=0)
    rs = (ctot + rt - 1) // rt * rt
    gend = jnp.cumsum(rs)
    gstart = gend - rs
    toff = jnp.cumsum(pc, axis=0) - pc
    urow = jnp.arange(MOE_NU, dtype=jnp.int32) * MOE_UNIT
    eu = jnp.sum((urow[None, :, None] >= lend[:, None, :]).astype(jnp.int32), axis=-1)
    valid = eu < e_ids
    run_start = jnp.max(jnp.where(lstart[:, None, :] <= urow[None, :, None], lstart[:, None, :], 0), axis=-1)
    uinfo = jnp.where(valid, eu * MOE_INFO_SHIFT + (urow[None, :] - run_start), e_ids * MOE_INFO_SHIFT)
    gbase = (gstart[None, :] + toff) // MOE_UNIT
    zn = ((rs - ctot) // MOE_UNIT).astype(jnp.int32)
    zdst = (gstart + ctot)[:, None] // MOE_UNIT + jnp.arange(rt // MOE_UNIT, dtype=jnp.int32)[None, :]
    n_used = (gend[-1] // rt).astype(jnp.int32).reshape(1)
    tmap = jnp.sum((jnp.arange(nt_max, dtype=jnp.int32)[:, None] * rt >= gend[None, :]).astype(jnp.int32), axis=-1)
    tmap = jnp.minimum(tmap, e_ids - 1)
    i32 = lambda a: a.astype(jnp.int32).reshape(-1)
    return dict(uinfo=i32(uinfo), gbase=i32(gbase),nun=nun, zn=zn, zdst=i32(zdst), tmap=i32(tmap), n_used=n_used)


def _routing_rank(sel, rank_s, wsel_s=None):
    tb = sel.shape[1]
    mask = sel > 0.0
    before = (lax.broadcasted_iota(jnp.int32, (tb, tb), 0) < lax.broadcasted_iota(jnp.int32, (tb, tb), 1))
    rank = jnp.dot(mask.astype(BF16), before.astype(BF16), preferred_element_type=F32)
    rank_s[0:N_EXPERTS, :] = jnp.where(mask, rank, -1.0)
    rank_s[N_EXPERTS:, :] = jnp.full((8, tb), -1.0, F32)
    if wsel_s is not None:
        wsel_s[0:N_EXPERTS, :] = sel
        wsel_s[N_EXPERTS:, :] = jnp.zeros((8, tb), F32)


def _unit_rows(uinfo_ref, gbase_ref, i, u, rank_s, gunit_s):
    info = uinfo_ref[i * MOE_NU + u]
    e = lax.shift_right_logical(info, MOE_INFO_BITS)
    off = jnp.bitwise_and(info, MOE_INFO_SHIFT - 1)
    gunit_s[u] = (gbase_ref[i * N_EXPERTS + jnp.minimum(e, N_EXPERTS - 1)]
                  + lax.shift_right_logical(off, MOE_UNIT_BITS))
    j = lax.broadcasted_iota(jnp.int32, (MOE_UNIT, MOE_TB), 0).astype(F32)
    return rank_s[pl.ds(e, 1), :] == (j + off.astype(F32)), e


def _for_each_unit(n, fn):
    def group(g, c):
        for k in range(MOE_ISSUE):
            fn(g * MOE_ISSUE + k)
        return c

    def single(u, c):
        fn(u)
        return c

    full = lax.shift_right_logical(n, MOE_ISSUE_BITS)
    lax.fori_loop(0, full, group, 0)
    lax.fori_loop(full * MOE_ISSUE, n, single, 0)


def _wait_units(n, src, dst, sem):
    @pl.when(n > 0)
    def _():
        rows = pl.ds(0, n * MOE_UNIT)
        pltpu.make_async_copy(src.at[rows], dst.at[rows], sem).wait()


def _unit_copy(src, dst, sem):
    return pltpu.make_async_copy(src, dst, sem)


def _moe_dispatch_kernel(rt, uinfo_ref, gbase_ref, nun_ref, zn_ref, zdst_ref, h_ref, sel_ref, xs_hbm,
                         p_s, xs_s0, xs_s1, rank_s, zero_s, gunit_s, sem0, sem1, zsem):
    i = pl.program_id(0)
    n = nun_ref[i]
    unit = lambda ref, u: ref.at[pl.ds(pl.multiple_of(u * MOE_UNIT, MOE_UNIT), MOE_UNIT)]

    @pl.when(i == 0)
    def _():
        zero_s[...] = jnp.zeros_like(zero_s)
        zper = rt // MOE_UNIT

        def per_expert(e, tot):
            def one(z, c):
                _unit_copy(zero_s, unit(xs_hbm, zdst_ref[e * zper + z]), zsem).start()
                return c
            lax.fori_loop(0, zn_ref[e], one, 0)
            return tot + zn_ref[e]

        total = lax.fori_loop(0, N_EXPERTS, per_expert, 0)

        def wait_one(z, c):
            _unit_copy(zero_s, unit(xs_hbm, 0), zsem).wait()
            return c
        lax.fori_loop(0, total, wait_one, 0)

    _routing_rank(sel_ref[0], rank_s)
    for u in range(MOE_NU):
        hit, _ = _unit_rows(uinfo_ref, gbase_ref, i, u, rank_s, gunit_s)
        p_s[u * MOE_UNIT:(u + 1) * MOE_UNIT, :] = hit.astype(BF16)

    h = h_ref[0]
    last = pl.num_programs(0) - 1

    def permute_and_send(xs_s, sem, other_s, other_sem):
        @pl.when(i >= 2)
        def _():
            _wait_units(nun_ref[jnp.maximum(i - 2, 0)], xs_s, xs_hbm, sem)

        for c in range(MOE_RMAX // MOE_CH):
            @pl.when(c * (MOE_CH // MOE_UNIT) < n)
            def _():
                rows = slice(c * MOE_CH, (c + 1) * MOE_CH)
                xs_s[rows, :] = jnp.dot(p_s[rows, :], h, preferred_element_type=F32).astype(BF16)

        _for_each_unit(n, lambda u: _unit_copy(unit(xs_s, u), unit(xs_hbm, gunit_s[u]), sem).start())

        @pl.when(i == last)
        def _():
            _wait_units(n, xs_s, xs_hbm, sem)

            @pl.when(i >= 1)
            def _():
                _wait_units(nun_ref[jnp.maximum(i - 1, 0)], other_s, xs_hbm, other_sem)

    @pl.when(jnp.bitwise_and(i, 1) == 0)
    def _():
        permute_and_send(xs_s0, sem0, xs_s1, sem1)

    @pl.when(jnp.bitwise_and(i, 1) == 1)
    def _():
        permute_and_send(xs_s1, sem1, xs_s0, sem0)


def _moe_dispatch(h3, sel, plan, pmax, rt):
    gsz, lg, _ = h3.shape
    tpg = lg // MOE_TB
    n_tiles = gsz * tpg
    grid_spec = pltpu.PrefetchScalarGridSpec(
        num_scalar_prefetch=5,
        grid=(n_tiles,),
        in_specs=[
            pl.BlockSpec((1, MOE_TB, D_MODEL), lambda i, *_: (i // tpg, i % tpg, 0)),
            pl.BlockSpec((1, N_EXPERTS, MOE_TB), lambda i, *_: (i, 0, 0)),
        ],
        out_specs=pl.BlockSpec(memory_space=pl.ANY),
        scratch_shapes=[
            pltpu.VMEM((MOE_RMAX, MOE_TB), BF16),
            pltpu.VMEM((MOE_RMAX, D_MODEL), BF16),
            pltpu.VMEM((MOE_RMAX, D_MODEL), BF16),
            pltpu.VMEM((N_EXPERTS + 8, MOE_TB), F32),
            pltpu.VMEM((MOE_UNIT, D_MODEL), BF16),
            pltpu.SMEM((MOE_NU,), jnp.int32),
            pltpu.SemaphoreType.DMA(()),
            pltpu.SemaphoreType.DMA(()),
            pltpu.SemaphoreType.DMA(()),
        ],
    )
    return pl.pallas_call(
        functools.partial(_moe_dispatch_kernel, rt),
        grid_spec=grid_spec,
        out_shape=jax.ShapeDtypeStruct((pmax, D_MODEL), BF16),
        compiler_params=_cparams(("arbitrary",)),
        name="moe_dispatch",
    )(plan['uinfo'], plan['gbase'], plan['nun'], plan['zn'], plan['zdst'], h3, sel)


def _moe_group_kernel(tmap_ref, nused_ref, xs_ref, wg_ref, wu_ref, wd_ref, ys_ref, wg_s, wu_s, wd_s):
    j = pl.program_id(0)

    @pl.when(j < nused_ref[0])
    def _():
        @pl.when(jnp.logical_or(j == 0, tmap_ref[j] != tmap_ref[jnp.maximum(j - 1, 0)]))
        def _():
            wg_s[...] = wg_ref[0, 0].astype(BF16)
            wu_s[...] = wu_ref[0, 0].astype(BF16)
            wd_s[...] = wd_ref[0, 0].astype(BF16)

        act = _swiglu_hidden(xs_ref[...], wg_s[...], wu_s[...])
        ys_ref[...] = jnp.dot(act.astype(BF16), wd_s[...], preferred_element_type=F32).astype(BF16)


def _moe_group(xs, plan, wg, wu, wd, layer, rt):
    pmax = xs.shape[0]
    nt = pmax // rt
    row_map = lambda j, tmap, nused: (jnp.maximum(jnp.minimum(j, nused[0] - 1), 0), 0)
    wmap = lambda j, tmap, nused: (layer, tmap[j], 0, 0)
    grid_spec = pltpu.PrefetchScalarGridSpec(
        num_scalar_prefetch=2,
        grid=(nt,),
        in_specs=[
            pl.BlockSpec((rt, D_MODEL), row_map),
            pl.BlockSpec((1, 1, D_MODEL, HIDDEN), wmap),
            pl.BlockSpec((1, 1, D_MODEL, HIDDEN), wmap),
            pl.BlockSpec((1, 1, HIDDEN, D_MODEL), wmap),
        ],
        out_specs=pl.BlockSpec((rt, D_MODEL), row_map),
        scratch_shapes=[
            pltpu.VMEM((D_MODEL, HIDDEN), BF16),
            pltpu.VMEM((D_MODEL, HIDDEN), BF16),
            pltpu.VMEM((HIDDEN, D_MODEL), BF16),
        ],
    )
    return pl.pallas_call(
        _moe_group_kernel,
        grid_spec=grid_spec,
        out_shape=jax.ShapeDtypeStruct((pmax, D_MODEL), BF16),
        compiler_params=_cparams(("arbitrary",)),
        name="moe_group",
    )(plan['tmap'], plan['n_used'], xs, wg, wu, wd)


def _moe_combine_kernel(final, uinfo_ref, gbase_ref, nun_ref, h_ref, sel_ref, x_ref, gate_ref, fn_ref,
                        sg_ref, su_ref, sd_ref, ys_hbm, o_ref, pw_s, ys_s, rank_s, wsel_s, acc_s, gunit_s, sem):
    i = pl.program_id(0)
    n = nun_ref[i]
    unit = lambda ref, u: ref.at[pl.ds(pl.multiple_of(u * MOE_UNIT, MOE_UNIT), MOE_UNIT)]

    @pl.when(i == 0)
    def _():
        ys_s[...] = jnp.zeros_like(ys_s)

    _routing_rank(sel_ref[0], rank_s, wsel_s)
    for u in range(MOE_NU):
        hit, e = _unit_rows(uinfo_ref, gbase_ref, i, u, rank_s, gunit_s)
        pw_s[u * MOE_UNIT:(u + 1) * MOE_UNIT, :] = jnp.where(hit, wsel_s[pl.ds(e, 1), :], 0.0).astype(BF16)

    _for_each_unit(n, lambda u: _unit_copy(unit(ys_hbm, gunit_s[u]), unit(ys_s, u), sem).start())

    h = h_ref[0]
    act = _swiglu_hidden(h, sg_ref[...], su_ref[...])
    acc_s[...] = jnp.dot(act.astype(BF16), sd_ref[...], preferred_element_type=F32)

    _wait_units(n, ys_hbm, ys_s, sem)

    for c in range(MOE_RMAX // MOE_CH):
        @pl.when(c * (MOE_CH // MOE_UNIT) < n)
        def _():
            rows = slice(c * MOE_CH, (c + 1) * MOE_CH)
            acc_s[...] += _dot_tn(pw_s[rows, :], ys_s[rows, :])

    y = x_ref[0] + gate_ref[0] * acc_s[...]
    if final:
        y = (y * lax.rsqrt(jnp.mean(y * y, axis=-1, keepdims=True) + EPS)) * fn_ref[...]
    o_ref[0] = y


def _moe_combine(x3, gate3, h3, sel, ys, plan, sg, su, sd, final_norm, final):
    gsz, lg, _ = x3.shape
    tpg = lg // MOE_TB
    lm = gate3.shape[1]
    tm = 1 if lm == 1 else MOE_TB
    tok_map = lambda i, *_: (i // tpg, i % tpg, 0)
    mod_map = (lambda i, *_: (i // tpg, 0, 0)) if lm == 1 else tok_map
    full = lambda a: pl.BlockSpec(a.shape, lambda i, *_: (0, 0))
    grid_spec = pltpu.PrefetchScalarGridSpec(
        num_scalar_prefetch=3,
        grid=(gsz * tpg,),
        in_specs=[
            pl.BlockSpec((1, MOE_TB, D_MODEL), tok_map),
            pl.BlockSpec((1, N_EXPERTS, MOE_TB), lambda i, *_: (i, 0, 0)),
            pl.BlockSpec((1, MOE_TB, D_MODEL), tok_map),
            pl.BlockSpec((1, tm, D_MODEL), mod_map),
            pl.BlockSpec((1, D_MODEL), lambda i, *_: (0, 0)),
            full(sg), full(su), full(sd),
            pl.BlockSpec(memory_space=pl.ANY),
        ],
        out_specs=pl.BlockSpec((1, MOE_TB, D_MODEL), tok_map),
        scratch_shapes=[
            pltpu.VMEM((MOE_RMAX, MOE_TB), BF16),
            pltpu.VMEM((MOE_RMAX, D_MODEL), BF16),
            pltpu.VMEM((N_EXPERTS + 8, MOE_TB), F32),
            pltpu.VMEM((N_EXPERTS + 8, MOE_TB), F32),
            pltpu.VMEM((MOE_TB, D_MODEL), F32),
            pltpu.SMEM((MOE_NU,), jnp.int32),
            pltpu.SemaphoreType.DMA(()),
        ],
    )
    return pl.pallas_call(
        functools.partial(_moe_combine_kernel, final),
        grid_spec=grid_spec,
        out_shape=jax.ShapeDtypeStruct(x3.shape, F32),
        compiler_params=_cparams(("arbitrary",)),
        name="moe_combine",
    )(plan['uinfo'], plan['gbase'], plan['nun'], h3, sel, x3, gate3, final_norm.reshape(1, D_MODEL), sg, su, sd, ys)


def _block_diag_pairs(s):
    bsz, nh, n, _ = s.shape
    s = s.reshape(bsz, nh // 2, 2, n, n)
    z = jnp.zeros_like(s[:, :, 0])
    top = jnp.concatenate([s[:, :, 0], z], axis=-1)
    bot = jnp.concatenate([z, s[:, :, 1]], axis=-1)
    return jnp.concatenate([top, bot], axis=-2)


def _unpair(s_bd):
    bsz, npair, n2, _ = s_bd.shape
    n = n2 // 2
    return jnp.stack([s_bd[:, :, :n, :n], s_bd[:, :, n:, n:]], axis=2).reshape(bsz, npair * 2, n, n)


def _rotary_tables(pos):
    half = RET_DK // 2
    inv = ROPE_BASE ** (-jnp.arange(half, dtype=F32) / half)
    ang = pos.astype(F32)[:, None] * inv[None, :]
    cos, sin = jnp.cos(ang), jnp.sin(ang)
    cos_t = jnp.concatenate([cos, cos, cos, cos], axis=-1)
    sin_t = jnp.concatenate([-sin, sin, -sin, sin], axis=-1)
    return cos_t, sin_t


def _trunk(x, c, pos, ret_s, rwkv_s, shift_s, cache, p, flat):
    bsz, seq, _ = x.shape
    mod = _ada(c, p['ada_w'], p['ada_b'])
    mod = mod.reshape(DEPTH, bsz, 6, D_MODEL)
    if flat:
        tok = lambda t: t.reshape(1, bsz * seq, t.shape[-1])
        untok = lambda t: t.reshape(bsz, seq, t.shape[-1])
        modv = lambda l, j: jnp.broadcast_to(mod[l, :, j][:, None, :], (bsz, seq, D_MODEL)).reshape(1, bsz * seq, D_MODEL)
        tl_proj = bsz * seq
    else:
        tok = untok = lambda t: t
        modv = lambda l, j: mod[l, :, j][:, None, :]
        tl_proj = min(256, seq)

    x3 = tok(x)
    w_in = p['w_in_ab'][0].astype(BF16)
    a_cols = 4 * RET_W
    zeros_a = jnp.zeros((1, a_cols), F32)
    zeros_b = jnp.zeros((1, B_COLS), F32)
    pa, pb = _norm_proj(x3, p['norm_mix'][0], modv(0, 0), modv(0, 1), [w_in[:, :a_cols], w_in[:, a_cols:]],
                        [zeros_a, zeros_b], tl_proj)
    pa, pb = untok(pa), untok(pb)
    cos_t, sin_t = _rotary_tables(pos)
    lgs = jnp.log1p(-jnp.exp2(-5.0 - jnp.arange(RET_HEADS, dtype=F32)))
    ret_chunk = math.gcd(seq, 128)
    o_a, ret_new = _retention(pa, cos_t, sin_t, lgs, p['ret_gn_w'][0], p['ret_gn_b'][0],
                              _block_diag_pairs(ret_s), ret_chunk)
    wlr = jnp.zeros((LOWRANK, 3 * RWKV_W), F32)
    wlr = wlr.at[0:64, 0:RWKV_W].set(p['rwkv_w_up'][0])
    wlr = wlr.at[64:128, RWKV_W:2 * RWKV_W].set(p['rwkv_a_up'][0])
    wlr = wlr.at[128:256, 2 * RWKV_W:].set(p['rwkv_g_up'][0])
    tb = min(seq, 128)
    o_b, rwkv_new = _rwkv(pb, shift_s[:, None, :], _block_diag_pairs(rwkv_s), p['rwkv_mu'][0], wlr.astype(BF16),
                          p['rwkv_w0'][0], p['rwkv_a0'][0], p['rwkv_k_k'][0], p['rwkv_k_a'][0],
                          p['rwkv_r_k'][0].reshape(-1), p['rwkv_ln_w'][0], p['rwkv_ln_b'][0],
                          tb, min(RWKV_CHUNK, seq))
    w_out = p['w_out_ab'][0].astype(BF16)
    x3 = _out_proj(x3, modv(0, 2), [tok(o_a), tok(o_b)], [w_out[:RET_W], w_out[RET_W:]], tl_proj)
    x3 = _moe_layer(x3, 0, modv, p, final=False)

    wq, wk, wv = jnp.split(p['w_qkv_c'][0].astype(BF16), 3, axis=1)
    q, k, v = _norm_proj(x3, p['norm_mix'][1], modv(1, 0), modv(1, 1), [wq, wk, wv],
                         [p['b_q_c'][0][None, :], p['b_k_c'][0][None, :], jnp.zeros((1, SB_W), F32)], tl_proj)
    q, k, v = untok(q), untok(k), untok(v)
    if cache is None:
        o = _sb_prompt(q, k, v, min(512, seq), min(256, seq))
    else:
        cache_k, cache_v, page_table = cache
        o = _sb_sample(q, k, v, cache_k, cache_v, page_table, 8)
    x3 = _out_proj(x3, modv(1, 2), [tok(o)], [p['w_out_c'][0].astype(BF16)], tl_proj)
    y3 = _moe_layer(x3, 1, modv, p, final=True)

    kv_shape = (1, bsz, seq, SB_HEADS, SB_DH)
    return (untok(y3), _unpair(ret_new)[None], _unpair(rwkv_new)[None], pb[:, -1][None],
            k.reshape(kv_shape), v.reshape(kv_shape))


def _moe_layer(x3, l, modv, p, final):
    gsz, lg, _ = x3.shape
    n_tiles = gsz * lg // MOE_TB
    rt = MOE_RT if n_tiles > 1 else MOE_RT_SMALL
    pmax = gsz * lg * TOP_K + n_tiles * N_EXPERTS * (MOE_UNIT - 1) + N_EXPERTS * (rt - MOE_UNIT)
    pmax = -(-pmax // rt) * rt
    h3, sel, cnt = _router(x3, p['norm_ffn'][l], modv(l, 3), modv(l, 4), p['router_w'][l], p['router_bias'][l])
    plan = _moe_plan(cnt[:, :, 0], pmax // rt, rt)
    xs = _moe_dispatch(h3, sel, plan, pmax, rt)
    ys = _moe_group(xs, plan, p['exp_w_gate'], p['exp_w_up'], p['exp_w_down'], l, rt)
    return _moe_combine(x3, modv(l, 5), h3, sel, ys, plan, p['sh_w_gate'][l].astype(BF16),
                        p['sh_w_up'][l].astype(BF16), p['sh_w_down'][l].astype(BF16), p['final_norm'], final)


def kernel(x_prompt, x_sample, c_prompt, c_sample, state_ret, state_rwkv, state_shift, cache_k, cache_v, page_table, ada_w, ada_b, norm_mix, norm_ffn, final_norm, w_in_ab, w_out_ab, ret_gn_w, ret_gn_b, rwkv_mu, rwkv_w0, rwkv_w_up, rwkv_a0, rwkv_a_up, rwkv_g_up, rwkv_k_k, rwkv_k_a, rwkv_r_k, rwkv_ln_w, rwkv_ln_b, w_qkv_c, b_q_c, b_k_c, w_out_c, router_w, router_bias, exp_w_gate, exp_w_up, exp_w_down, sh_w_gate, sh_w_up, sh_w_down):
    p = dict(ada_w=ada_w, ada_b=ada_b, norm_mix=norm_mix, norm_ffn=norm_ffn, final_norm=final_norm,
             w_in_ab=w_in_ab, w_out_ab=w_out_ab, ret_gn_w=ret_gn_w, ret_gn_b=ret_gn_b, rwkv_mu=rwkv_mu,
             rwkv_w0=rwkv_w0, rwkv_w_up=rwkv_w_up, rwkv_a0=rwkv_a0, rwkv_a_up=rwkv_a_up, rwkv_g_up=rwkv_g_up,
             rwkv_k_k=rwkv_k_k, rwkv_k_a=rwkv_k_a, rwkv_r_k=rwkv_r_k, rwkv_ln_w=rwkv_ln_w, rwkv_ln_b=rwkv_ln_b,
             w_qkv_c=w_qkv_c, b_q_c=b_q_c, b_k_c=b_k_c, w_out_c=w_out_c, router_w=router_w,
             router_bias=router_bias, exp_w_gate=exp_w_gate, exp_w_up=exp_w_up, exp_w_down=exp_w_down,
             sh_w_gate=sh_w_gate, sh_w_up=sh_w_up, sh_w_down=sh_w_down)
    bp, lp, _ = x_prompt.shape
    bs, ls, _ = x_sample.shape
    zeros = lambda *s: jnp.zeros(s, F32)
    y_p, ret_p, rwkv_p, shift_p, k_p, v_p = _trunk(
        x_prompt, c_prompt, jnp.arange(lp), zeros(bp, RET_HEADS, RET_DK, RET_DK),
        zeros(bp, RWKV_HEADS, RWKV_N, RWKV_N), zeros(bp, B_COLS), None, p, flat=False)
    n_pages = page_table.shape[1]
    n_pool = cache_k.shape[1]
    pages = lambda t: jnp.transpose(t, (0, 1, 3, 4, 2)).reshape(1, n_pool, SB_W, PAGE)
    cache = (pages(cache_k), pages(cache_v), page_table)
    y_s, ret_s, rwkv_s, shift_s, k_s, v_s = _trunk(
        x_sample, c_sample, n_pages * PAGE + jnp.arange(ls), state_ret[0], state_rwkv[0], state_shift[0],
        cache, p, flat=True)
    return (y_p, y_s, ret_p, ret_s, rwkv_p, rwkv_s, shift_p, shift_s, k_p, v_p, k_s, v_s)
```
